```python
import math
import jax, jax.numpy as jnp
from jax import lax
import numpy as np

D_MODEL = 1024
BATCH = 4
SEQ = 4096
DEPTH = 2

GRID_W = 64
CTX_LEN = 256
Q_BLOCK = 128
EPS = 1e-6
ROPE_THETA = 10000.0

A_HEAD_DIM = 64
A_HEADS = D_MODEL // 128
A_KV_HEADS = A_HEADS // 4
A_GROUP = A_HEADS // A_KV_HEADS
A_Q = A_HEADS * A_HEAD_DIM
A_KV = A_KV_HEADS * A_HEAD_DIM
A_OUT = A_Q

B_CHANNELS = D_MODEL // 2
B_CONV_W = 31
B_IN = 2 * B_CHANNELS

C_HEADS = D_MODEL // 256
C_HEAD_DIM = 64
C_V_DIM = 2 * C_HEAD_DIM
C_QK = C_HEADS * 2 * C_HEAD_DIM
C_V = C_HEADS * C_V_DIM

N_BRANCH = 3
IN_SPLITS = [A_Q, A_KV, A_KV, B_IN, C_QK, C_QK, C_V, N_BRANCH * D_MODEL]
IN_WIDTH = sum(IN_SPLITS)
IN_OFFSETS = [int(v) for v in np.cumsum(IN_SPLITS)[:-1]]

D_FF = 256 * math.ceil(8 * D_MODEL / 3 / 256)
N_EXPERTS = 8
TOP_K = 2
D_FF_EXPERT = 7 * D_MODEL // 2
N_DENSE = (DEPTH + 1) // 2
N_MOE = DEPTH // 2

kernel_name = "hybrid_gqa_conformer_diffattn_moe_dit"


def rms_norm(x, g):
    xf = x.astype(jnp.float32)
    y = xf * lax.rsqrt(jnp.mean(xf * xf, axis=-1, keepdims=True) + EPS)
    return (y * g.astype(jnp.float32)).astype(x.dtype)


def layer_norm(x, g, b):
    xf = x.astype(jnp.float32)
    mu = jnp.mean(xf, axis=-1, keepdims=True)
    var = jnp.mean(jnp.square(xf - mu), axis=-1, keepdims=True)
    y = (xf - mu) * lax.rsqrt(var + EPS)
    return (y * g.astype(jnp.float32) + b.astype(jnp.float32)).astype(x.dtype)


def axial_rope_tables(seq, head_dim):
    rows = seq // GRID_W
    row = jnp.repeat(jnp.arange(rows), GRID_W)
    col = jnp.tile(jnp.arange(GRID_W), rows)
    n_freq = head_dim // 4
    inv = ROPE_THETA ** (-jnp.arange(n_freq, dtype=jnp.float32) / n_freq)
    pos = jnp.stack([row, col], axis=-1).astype(jnp.float32)
    ang = pos[:, :, None] * inv
    return jnp.cos(ang), jnp.sin(ang)


def apply_axial_rope(x, cos, sin):
    shp = x.shape
    xr = x.astype(jnp.float32).reshape(shp[:-1] + (2, 2, shp[-1] // 4))
    extra = x.ndim - 3
    cos = cos.reshape((cos.shape[0],) + (1,) * extra + cos.shape[1:])
    sin = sin.reshape((sin.shape[0],) + (1,) * extra + sin.shape[1:])
    x1, x2 = xr[..., 0, :], xr[..., 1, :]
    out = jnp.stack([x1 * cos - x2 * sin, x2 * cos + x1 * sin], axis=-2)
    return out.reshape(shp).astype(x.dtype)


def map_query_blocks(fn, q):
    b, s = q.shape[0], q.shape[1]
    qb = jnp.moveaxis(q.reshape((b, s // Q_BLOCK, Q_BLOCK) + q.shape[2:]), 1, 0)
    out = lax.map(fn, qb)
    out = jnp.moveaxis(out, 0, 1)
    return out.reshape((b, s) + out.shape[3:])


def gqa_attend(q, k, v):
    s = jnp.einsum('bqkgd,bskd->bkgqs', q, k).astype(jnp.float32) * (A_HEAD_DIM ** -0.5)
    p = jax.nn.softmax(s, axis=-1).astype(v.dtype)
    return jnp.einsum('bkgqs,bskd->bqkgd', p, v)


def diff_attend(q, k, v, lam):
    s = jnp.einsum('bqhmd,bshmd->bhmqs', q, k).astype(jnp.float32) * (C_HEAD_DIM ** -0.5)
    p = jax.nn.softmax(s, axis=-1)
    a = (p[:, :, 0] - lam * p[:, :, 1]).astype(v.dtype)
    return jnp.einsum('bhqs,bshd->bqhd', a, v)


def project_in(u, qn_g, kn_g):
    aq, ak, av, bz, cq, ck, cv, gt = jnp.split(u, IN_OFFSETS, axis=-1)
    b, t = u.shape[0], u.shape[1]
    aq = rms_norm(aq.reshape(b, t, A_KV_HEADS, A_GROUP, A_HEAD_DIM), qn_g)
    ak = rms_norm(ak.reshape(b, t, A_KV_HEADS, A_HEAD_DIM), kn_g)
    av = av.reshape(b, t, A_KV_HEADS, A_HEAD_DIM)
    cq = cq.reshape(b, t, C_HEADS, 2, C_HEAD_DIM)
    ck = ck.reshape(b, t, C_HEADS, 2, C_HEAD_DIM)
    cv = cv.reshape(b, t, C_HEADS, C_V_DIM)
    return aq, ak, av, bz, cq, ck, cv, gt


def conformer_conv(z, dw_w, dw_b, ln_g, ln_b):
    a, g = jnp.split(z, 2, axis=-1)
    y = a * jax.nn.sigmoid(g)
    pad = B_CONV_W // 2
    y = lax.conv_general_dilated(y, dw_w[:, None, :], window_strides=(1,), padding=[(pad, pad)],
                                 dimension_numbers=('NWC', 'WIO', 'NWC'),
                                 feature_group_count=B_CHANNELS) + dw_b
    return jax.nn.silu(layer_norm(y, ln_g, ln_b))


def mixer_merge(o_a, o_c, bz, gt, lam_init, subln_g, dw_w, dw_b, ln_g, ln_b, b_gate, w_pa, w_pb, w_pc, w_out):
    b, t = o_a.shape[0], o_a.shape[1]
    o_a = o_a.reshape(b, t, A_OUT)
    o_c = (rms_norm(o_c, subln_g) * (1.0 - lam_init)).reshape(b, t, C_V)
    o_b = conformer_conv(bz, dw_w, dw_b, ln_g, ln_b)
    g_a, g_b, g_c = jnp.split(jax.nn.sigmoid(gt + b_gate), N_BRANCH, axis=-1)
    m = g_a * (o_a @ w_pa) + g_b * (o_b @ w_pb) + g_c * (o_c @ w_pc)
    return m @ w_out


def swiglu(h, w1, w3, w2):
    return (jax.nn.silu(h @ w1) * (h @ w3)) @ w2


def moe_swiglu(h, w_r, b_r, w1, w3, w2):
    logits = (h @ w_r).astype(jnp.float32) + b_r.astype(jnp.float32)
    top_v, top_i = lax.top_k(logits, TOP_K)
    top_w = jax.nn.softmax(top_v, axis=-1)
    gates = jnp.sum(jax.nn.one_hot(top_i, N_EXPERTS, dtype=jnp.float32) * top_w[..., None], axis=-2).astype(h.dtype)
    out = jnp.zeros_like(h)
    for e in range(N_EXPERTS):
        out = out + gates[..., e:e + 1] * swiglu(h, w1[e], w3[e], w2[e])
    return out


def setup_inputs(seed: int = 0) -> dict:
    key = jax.random.key(seed)
    ks = iter(jax.random.split(key, 48))

    def nrm(shape, scale):
        return jax.random.normal(next(ks), shape, jnp.float32) * scale

    d = D_MODEL
    return {
        "x": nrm((BATCH, SEQ, d), 1.0),
        "c": nrm((BATCH, d), 1.0),
        "ctx": nrm((BATCH, CTX_LEN, d), 1.0),
        "c_ctx": nrm((d,), 1.0),
        "w_mod": nrm((DEPTH, d, 6 * d), 0.5 * d ** -0.5),
        "b_mod": nrm((DEPTH, 6 * d), 0.02),
        "norm1_g": 1.0 + nrm((DEPTH, d), 0.02),
        "norm2_g": 1.0 + nrm((DEPTH, d), 0.02),
        "w_in": nrm((DEPTH, d, IN_WIDTH), d ** -0.5),
        "b_gate": nrm((DEPTH, N_BRANCH * d), 0.02),
        "a_qn_g": 1.0 + nrm((DEPTH, A_HEAD_DIM), 0.02),
        "a_kn_g": 1.0 + nrm((DEPTH, A_HEAD_DIM), 0.02),
        "b_dw_w": nrm((DEPTH, B_CONV_W, B_CHANNELS), B_CONV_W ** -0.5),
        "b_dw_b": nrm((DEPTH, B_CHANNELS), 0.02),
        "b_ln_g": 1.0 + nrm((DEPTH, B_CHANNELS), 0.02),
        "b_ln_b": nrm((DEPTH, B_CHANNELS), 0.02),
        "c_lq1": nrm((DEPTH, C_HEAD_DIM), 0.1),
        "c_lk1": nrm((DEPTH, C_HEAD_DIM), 0.1),
        "c_lq2": nrm((DEPTH, C_HEAD_DIM), 0.1),
        "c_lk2": nrm((DEPTH, C_HEAD_DIM), 0.1),
        "c_subln_g": 1.0 + nrm((DEPTH, C_V_DIM), 0.02),
        "w_pa": nrm((DEPTH, A_OUT, d), A_OUT ** -0.5),
        "w_pb": nrm((DEPTH, B_CHANNELS, d), B_CHANNELS ** -0.5),
        "w_pc": nrm((DEPTH, C_V, d), C_V ** -0.5),
        "w_out": nrm((DEPTH, d, d), d ** -0.5),
        "ffn_w1": nrm((N_DENSE, d, D_FF), d ** -0.5),
        "ffn_w3": nrm((N_DENSE, d, D_FF), d ** -0.5),
        "ffn_w2": nrm((N_DENSE, D_FF, d), D_FF ** -0.5),
        "moe_router": nrm((N_MOE, d, N_EXPERTS), d ** -0.5),
        "moe_router_b": nrm((N_MOE, N_EXPERTS), 0.01),
        "moe_w1": nrm((N_MOE, N_EXPERTS, d, D_FF_EXPERT), d ** -0.5),
        "moe_w3": nrm((N_MOE, N_EXPERTS, d, D_FF_EXPERT), d ** -0.5),
        "moe_w2": nrm((N_MOE, N_EXPERTS, D_FF_EXPERT, d), D_FF_EXPERT ** -0.5),
        "final_g": 1.0 + nrm((d,), 0.02),
    }


def reference(x, c, ctx, c_ctx, w_mod, b_mod, norm1_g, norm2_g, w_in, b_gate,
              a_qn_g, a_kn_g, b_dw_w, b_dw_b, b_ln_g, b_ln_b,
              c_lq1, c_lk1, c_lq2, c_lk2, c_subln_g,
              w_pa, w_pb, w_pc, w_out,
              ffn_w1, ffn_w3, ffn_w2,
              moe_router, moe_router_b, moe_w1, moe_w3, moe_w2, final_g):
    seq = x.shape[1]
    cos, sin = axial_rope_tables(seq, A_HEAD_DIM)
    xc = ctx
    for i in range(DEPTH):
        last = i == DEPTH - 1
        mod = jnp.split((jax.nn.silu(c) @ w_mod[i] + b_mod[i])[:, None, :], 6, axis=-1)
        modc = jnp.split((jax.nn.silu(c_ctx) @ w_mod[i] + b_mod[i])[None, None, :], 6, axis=-1)

        h = rms_norm(x, norm1_g[i]) * (1.0 + mod[1]) + mod[0]
        hc = rms_norm(xc, norm1_g[i]) * (1.0 + modc[1]) + modc[0]
        aq, ak, av, bz, cq, ck, cv, gt = project_in(h @ w_in[i], a_qn_g[i], a_kn_g[i])
        aqc, akc, avc, bzc, cqc, ckc, cvc, gtc = project_in(hc @ w_in[i], a_qn_g[i], a_kn_g[i])
        aq = apply_axial_rope(aq, cos, sin)
        ak = apply_axial_rope(ak, cos, sin)
        cq = apply_axial_rope(cq, cos, sin)
        ck = apply_axial_rope(ck, cos, sin)

        lam_init = 0.8 - 0.6 * math.exp(-0.3 * i)
        lam = (jnp.exp(jnp.sum(c_lq1[i].astype(jnp.float32) * c_lk1[i].astype(jnp.float32)))
               - jnp.exp(jnp.sum(c_lq2[i].astype(jnp.float32) * c_lk2[i].astype(jnp.float32)))
               + lam_init)

        k_a = jnp.concatenate([ak, akc], axis=1)
        v_a = jnp.concatenate([av, avc], axis=1)
        o_a = map_query_blocks(lambda qb: gqa_attend(qb, k_a, v_a), aq)
        k_c = jnp.concatenate([ck, ckc], axis=1)
        v_c = jnp.concatenate([cv, cvc], axis=1)
        o_c = map_query_blocks(lambda qb: diff_attend(qb, k_c, v_c, lam), cq)
        y = mixer_merge(o_a, o_c, bz, gt, lam_init, c_subln_g[i], b_dw_w[i], b_dw_b[i], b_ln_g[i], b_ln_b[i],
                        b_gate[i], w_pa[i], w_pb[i], w_pc[i], w_out[i])
        x = x + mod[2] * y

        if not last:
            o_ac = gqa_attend(aqc, akc, avc)
            o_cc = diff_attend(cqc, ckc, cvc, lam)
            yc = mixer_merge(o_ac, o_cc, bzc, gtc, lam_init, c_subln_g[i], b_dw_w[i], b_dw_b[i], b_ln_g[i],
                             b_ln_b[i], b_gate[i], w_pa[i], w_pb[i], w_pc[i], w_out[i])
            xc = xc + modc[2] * yc

        j = i // 2
        h2 = rms_norm(x, norm2_g[i]) * (1.0 + mod[4]) + mod[3]
        if i % 2 == 0:
            x = x + mod[5] * swiglu(h2, ffn_w1[j], ffn_w3[j], ffn_w2[j])
        else:
            x = x + mod[5] * moe_swiglu(h2, moe_router[j], moe_router_b[j], moe_w1[j], moe_w3[j], moe_w2[j])
        if not last:
            h2c = rms_norm(xc, norm2_g[i]) * (1.0 + modc[4]) + modc[3]
            if i % 2 == 0:
                xc = xc + modc[5] * swiglu(h2c, ffn_w1[j], ffn_w3[j], ffn_w2[j])
            else:
                xc = xc + modc[5] * moe_swiglu(h2c, moe_router[j], moe_router_b[j], moe_w1[j], moe_w3[j],
                                               moe_w2[j])
    return rms_norm(x, final_g)
```

```python
import functools
import math

import numpy as np
import jax
import jax.numpy as jnp
from jax import lax
from jax.experimental import pallas as pl
from jax.experimental.pallas import tpu as pltpu

F32 = jnp.float32
BF16 = jnp.bfloat16

D_MODEL = 1024
BATCH = 4
SEQ = 4096
DEPTH = 2
GRID_W = 64
CTX_LEN = 256
EPS = 1e-6
ROPE_THETA = 10000.0

HEAD_DIM = 64
A_HEADS = 8
A_KV_HEADS = 2
A_GROUP = 4
A_Q = 512
A_KV = 128
B_CHANNELS = 512
B_CONV_W = 31
C_HEADS = 4
C_V_DIM = 128
C_QK = 512
C_V = 512
N_BRANCH = 3
IN_SPLITS = [A_Q, A_KV, A_KV, 2 * B_CHANNELS, C_QK, C_QK, C_V, N_BRANCH * D_MODEL]
IN_WIDTH = sum(IN_SPLITS)
IN_OFF = [0] + [int(v) for v in np.cumsum(IN_SPLITS)]
D_FF = 2816
N_EXPERTS = 8
D_FF_EXPERT = 3584

LANES = 128
ONES_ROWS = 16
KEY_CHUNK = 512
VMEM_LIMIT = 56 * 1024 * 1024


def _cparams(*sem):
    return pltpu.CompilerParams(dimension_semantics=sem, vmem_limit_bytes=VMEM_LIMIT)


def _resident(shape):
    nd = len(shape)
    return pl.BlockSpec(shape, lambda *_: (0,) * nd, pipeline_mode=pl.Buffered(1))


def _sigmoid(v):
    return 1.0 / (1.0 + jnp.exp(-v))


def _silu(v):
    return v * _sigmoid(v)


def _norm_mod(x, g, shift, scale):
    ms = jnp.mean(x * x, axis=-1, keepdims=True)
    return (x * lax.rsqrt(ms + EPS) * g) * (1.0 + scale) + shift


def _dot(a, b):
    return jnp.dot(a, b, preferred_element_type=F32)


def _mod_kernel(c_ref, w_ref, b_ref, o_ref):
    s = _silu(c_ref[...])
    o_ref[0] = jnp.dot(s, w_ref[0], preferred_element_type=F32, precision=lax.Precision.HIGHEST) + b_ref[0]


def _mod_call(cc, w_mod, b_mod):
    n = 6 * D_MODEL
    tn = D_MODEL
    return pl.pallas_call(
        _mod_kernel,
        grid=(DEPTH, n // tn),
        in_specs=[
            pl.BlockSpec((8, D_MODEL), lambda i, j: (0, 0)),
            pl.BlockSpec((1, D_MODEL, tn), lambda i, j: (i, 0, j)),
            pl.BlockSpec((1, 1, tn), lambda i, j: (i, 0, j)),
        ],
        out_specs=pl.BlockSpec((1, 8, tn), lambda i, j: (i, 0, j)),
        out_shape=jax.ShapeDtypeStruct((DEPTH, 8, n), F32),
        compiler_params=_cparams("arbitrary", "arbitrary"),
        name="adaln_mod",
    )(cc, w_mod, b_mod.reshape(DEPTH, 1, n))


def _segment_mean_sq(u, bd):
    sq = u * u
    hi = sq.astype(BF16)
    lo = (sq - hi.astype(F32)).astype(BF16)
    return (_dot(hi, bd) + _dot(lo, bd)) * (1.0 / HEAD_DIM)


def _rope_group(xg, cos, sin, lo_mask):
    partner = jnp.where(lo_mask, pltpu.roll(xg, LANES - 16, 1), pltpu.roll(xg, 16, 1))
    return xg * cos + partner * sin


def _inproj_kernel(x_ref, mod_ref, g1_ref, w_ref, bgate_ref, qn_ref, kn_ref, bd_ref, cos_ref, sin_ref,
                   qa_ref, ka_ref, vta_ref, yb_ref, qc_ref, kc_ref, vtc_ref, gt_ref, *, rope, tm):
    mod = mod_ref[0]
    h = _norm_mod(x_ref[...], g1_ref[...], mod[:, 0:D_MODEL], mod[:, D_MODEL:2 * D_MODEL]).astype(BF16)

    def proj(c0, c1):
        return _dot(h, w_ref[:, c0:c1])

    if rope:
        cos = cos_ref[...]
        sin = sin_ref[...]
        lane = lax.broadcasted_iota(jnp.int32, (tm, LANES), 1)
        lo_mask = (lane % 32) < 16

    def rot(u, j):
        ug = u[:, j * LANES:(j + 1) * LANES]
        return _rope_group(ug, cos, sin, lo_mask) if rope else ug

    ones_pat = jnp.where(lax.broadcasted_iota(jnp.int32, (ONES_ROWS, tm), 0) == 0, 1.0, 0.0).astype(BF16)

    u = proj(IN_OFF[0], IN_OFF[1])
    u = u * lax.rsqrt(_segment_mean_sq(u, bd_ref[...]) + EPS) * (qn_ref[...] * HEAD_DIM ** -0.5)
    for j in range(A_Q // LANES):
        qa_ref[:, j * LANES:(j + 1) * LANES] = rot(u, j).astype(BF16)
    u = proj(IN_OFF[1], IN_OFF[2])
    u = u * lax.rsqrt(_segment_mean_sq(u, bd_ref[0:A_KV, 0:A_KV]) + EPS) * kn_ref[...]
    ka_ref[...] = rot(u, 0).astype(BF16)
    ut = proj(IN_OFF[2], IN_OFF[3]).T
    for j in range(A_KV_HEADS):
        vta_ref[0, j, 0, 0:HEAD_DIM, :] = ut[j * HEAD_DIM:(j + 1) * HEAD_DIM].astype(BF16)
        vta_ref[0, j, 0, HEAD_DIM:HEAD_DIM + ONES_ROWS, :] = ones_pat
    a = proj(IN_OFF[3], IN_OFF[3] + B_CHANNELS)
    g = proj(IN_OFF[3] + B_CHANNELS, IN_OFF[4])
    yb_ref[...] = a * _sigmoid(g)
    u = proj(IN_OFF[4], IN_OFF[5]) * HEAD_DIM ** -0.5
    for j in range(C_QK // LANES):
        qc_ref[:, j * LANES:(j + 1) * LANES] = rot(u, j).astype(BF16)
    u = proj(IN_OFF[5], IN_OFF[6])
    for j in range(C_QK // LANES):
        kc_ref[:, j * LANES:(j + 1) * LANES] = rot(u, j).astype(BF16)
    ut = proj(IN_OFF[6], IN_OFF[7]).T
    for j in range(C_HEADS):
        vtc_ref[0, j, 0, 0:C_V_DIM, :] = ut[j * C_V_DIM:(j + 1) * C_V_DIM].astype(BF16)
        vtc_ref[0, j, 0, C_V_DIM:C_V_DIM + ONES_ROWS, :] = ones_pat
    for j in range(N_BRANCH):
        c0 = IN_OFF[7] + j * D_MODEL
        gt_ref[:, j * D_MODEL:(j + 1) * D_MODEL] = _sigmoid(
            proj(c0, c0 + D_MODEL) + bgate_ref[:, j * D_MODEL:(j + 1) * D_MODEL]).astype(BF16)


def _inproj_call(x, mod3, mod_row, g1, w_in, b_gate, qn, kn, bd, cos_t, sin_t, *, tokens_per_batch, rope):
    p = tokens_per_batch
    tm = min(p, KEY_CHUNK)
    nblk = p // tm
    t = BATCH * p
    row = lambda i: (i, 0)
    vt_map = lambda i: (i // nblk, 0, i % nblk, 0, 0)
    out_shape = (
        jax.ShapeDtypeStruct((t, A_Q), BF16),
        jax.ShapeDtypeStruct((t, A_KV), BF16),
        jax.ShapeDtypeStruct((BATCH, A_KV_HEADS, nblk, HEAD_DIM + ONES_ROWS, tm), BF16),
        jax.ShapeDtypeStruct((t, B_CHANNELS), F32),
        jax.ShapeDtypeStruct((t, C_QK), BF16),
        jax.ShapeDtypeStruct((t, C_QK), BF16),
        jax.ShapeDtypeStruct((BATCH, C_HEADS, nblk, C_V_DIM + ONES_ROWS, tm), BF16),
        jax.ShapeDtypeStruct((t, N_BRANCH * D_MODEL), BF16),
    )
    out_specs = (
        pl.BlockSpec((tm, A_Q), row),
        pl.BlockSpec((tm, A_KV), row),
        pl.BlockSpec((1, A_KV_HEADS, 1, HEAD_DIM + ONES_ROWS, tm), vt_map),
        pl.BlockSpec((tm, B_CHANNELS), row),
        pl.BlockSpec((tm, C_QK), row),
        pl.BlockSpec((tm, C_QK), row),
        pl.BlockSpec((1, C_HEADS, 1, C_V_DIM + ONES_ROWS, tm), vt_map),
        pl.BlockSpec((tm, N_BRANCH * D_MODEL), row),
    )
    in_specs = [
        pl.BlockSpec((tm, D_MODEL), row),
        pl.BlockSpec((1, 1, 6 * D_MODEL), lambda i: (mod_row(i), 0, 0)),
        _resident((1, D_MODEL)),
        _resident((D_MODEL, IN_WIDTH)),
        _resident((1, N_BRANCH * D_MODEL)),
        _resident((1, A_Q)),
        _resident((1, A_KV)),
        _resident((A_Q, A_Q)),
        pl.BlockSpec((tm, LANES), lambda i: (i % nblk, 0)),
        pl.BlockSpec((tm, LANES), lambda i: (i % nblk, 0)),
    ]
    return pl.pallas_call(
        functools.partial(_inproj_kernel, rope=rope, tm=tm),
        grid=(t // tm,),
        in_specs=in_specs,
        out_specs=out_specs,
        out_shape=out_shape,
        compiler_params=_cparams("parallel"),
        name="inproj_rope" if rope else "inproj_ctx",
    )(x, mod3, g1, w_in, b_gate, qn, kn, bd, cos_t, sin_t)


def _attend_pass(q_ext, k_lat_ref, k_ctx_ref, vt_lat_ref, vt_ctx_ref, lane0, vgroup, dvx, tq):
    n_chunks = k_lat_ref.shape[1] // KEY_CHUNK
    dn = (((1,), (1,)), ((), ()))

    def step(k, vt, carry):
        m, acc = carry
        st = lax.dot_general(k, q_ext, dn, preferred_element_type=F32)
        m_new = jnp.maximum(m, jnp.max(st, axis=0, keepdims=True))
        p = jnp.exp(st - m_new).astype(BF16)
        acc = jnp.exp(m - m_new) * acc + _dot(vt, p)
        return m_new, acc

    def body(c, carry):
        r0 = pl.multiple_of(c * KEY_CHUNK, KEY_CHUNK)
        k = k_lat_ref[0, pl.ds(r0, KEY_CHUNK), lane0:lane0 + LANES]
        return step(k, vt_lat_ref[0, vgroup, c], carry)

    carry = (jnp.full((1, tq), -jnp.inf, F32), jnp.zeros((dvx, tq), F32))
    carry = lax.fori_loop(0, n_chunks, body, carry)
    _, acc = step(k_ctx_ref[0, :, lane0:lane0 + LANES], vt_ctx_ref[0, vgroup, 0], carry)
    return acc


def _half_queries(q_ref, group, tq):
    qg = q_ref[:, group * LANES:(group + 1) * LANES]
    lane = lax.broadcasted_iota(jnp.int32, (tq, LANES), 1)
    zero = jnp.zeros_like(qg)
    return jnp.where(lane < HEAD_DIM, qg, zero), jnp.where(lane >= HEAD_DIM, qg, zero)


def _gqa_kernel(q_ref, k_lat_ref, k_ctx_ref, vt_lat_ref, vt_ctx_ref, o_ref, *, tq):
    dvx = HEAD_DIM + ONES_ROWS
    for pair in range(A_GROUP):
        outs = []
        for half, q_ext in enumerate(_half_queries(q_ref, pair, tq)):
            acc = _attend_pass(q_ext, k_lat_ref, k_ctx_ref, vt_lat_ref, vt_ctx_ref, 0, half, dvx, tq)
            outs.append(acc[0:HEAD_DIM] / acc[HEAD_DIM:HEAD_DIM + 1])
        o_ref[:, pair * LANES:(pair + 1) * LANES] = jnp.concatenate(outs, axis=0).T.astype(BF16)


def _diff_kernel(q_ref, k_lat_ref, k_ctx_ref, vt_lat_ref, vt_ctx_ref, lq1_ref, lk1_ref, lq2_ref, lk2_ref,
                 subg_ref, o_ref, *, tq, lam_init):
    dvx = C_V_DIM + ONES_ROWS
    lam = (jnp.exp(jnp.sum(lq1_ref[...] * lk1_ref[...], axis=-1, keepdims=True))
           - jnp.exp(jnp.sum(lq2_ref[...] * lk2_ref[...], axis=-1, keepdims=True)) + lam_init)
    for head in range(C_HEADS):
        outs = []
        for q_ext in _half_queries(q_ref, head, tq):
            acc = _attend_pass(q_ext, k_lat_ref, k_ctx_ref, vt_lat_ref, vt_ctx_ref, head * LANES, head, dvx, tq)
            outs.append(acc[0:C_V_DIM] / acc[C_V_DIM:C_V_DIM + 1])
        o = outs[0] - lam * outs[1]
        o = o * lax.rsqrt(jnp.mean(o * o, axis=0, keepdims=True) + EPS)
        o_ref[:, head * LANES:(head + 1) * LANES] = (o.T * subg_ref[...] * (1.0 - lam_init)).astype(BF16)


def _attention_call(kernel, name, q, k_lat, k_ctx, vt_lat, vt_ctx, extra, *, tokens_per_batch, tq):
    nq = tokens_per_batch // tq
    kw = k_lat.shape[-1]
    g, dvx = vt_lat.shape[1], vt_lat.shape[3]
    bmap3 = lambda b, i: (b, 0, 0)
    bmap5 = lambda b, i: (b, 0, 0, 0, 0)
    in_specs = [
        pl.BlockSpec((tq, q.shape[1]), lambda b, i: (b * nq + i, 0)),
        pl.BlockSpec((1, k_lat.shape[1], kw), bmap3),
        pl.BlockSpec((1, k_ctx.shape[1], kw), bmap3),
        pl.BlockSpec((1, g, vt_lat.shape[2], dvx, vt_lat.shape[4]), bmap5),
        pl.BlockSpec((1, g, 1, dvx, vt_ctx.shape[4]), bmap5),
    ] + [_resident(e.shape) for e in extra]
    return pl.pallas_call(
        kernel,
        grid=(BATCH, nq),
        in_specs=in_specs,
        out_specs=pl.BlockSpec((tq, q.shape[1]), lambda b, i: (b * nq + i, 0)),
        out_shape=jax.ShapeDtypeStruct(q.shape, BF16),
        compiler_params=_cparams("parallel", "arbitrary"),
        name=name,
    )(q, k_lat, k_ctx, vt_lat, vt_ctx, *extra)


def _attend_ctx(q_ext, k, vt):
    dn = (((1,), (1,)), ((), ()))
    st = lax.dot_general(k, q_ext, dn, preferred_element_type=F32)
    p = jnp.exp(st - jnp.max(st, axis=0, keepdims=True)).astype(BF16)
    return _dot(vt, p)


def _gqa_ctx_kernel(q_ref, k_ref, vt_ref, o_ref, *, tq):
    for pair in range(A_GROUP):
        outs = []
        for half, q_ext in enumerate(_half_queries(q_ref, pair, tq)):
            acc = _attend_ctx(q_ext, k_ref[0], vt_ref[0, half, 0])
            outs.append(acc[0:HEAD_DIM] / acc[HEAD_DIM:HEAD_DIM + 1])
        o_ref[:, pair * LANES:(pair + 1) * LANES] = jnp.concatenate(outs, axis=0).T.astype(BF16)


def _diff_ctx_kernel(q_ref, k_ref, vt_ref, lq1_ref, lk1_ref, lq2_ref, lk2_ref, subg_ref, o_ref, *, tq, lam_init):
    lam = (jnp.exp(jnp.sum(lq1_ref[...] * lk1_ref[...], axis=-1, keepdims=True))
           - jnp.exp(jnp.sum(lq2_ref[...] * lk2_ref[...], axis=-1, keepdims=True)) + lam_init)
    for head in range(C_HEADS):
        outs = []
        for q_ext in _half_queries(q_ref, head, tq):
            acc = _attend_ctx(q_ext, k_ref[0, :, head * LANES:(head + 1) * LANES], vt_ref[0, head, 0])
            outs.append(acc[0:C_V_DIM] / acc[C_V_DIM:C_V_DIM + 1])
        o = outs[0] - lam * outs[1]
        o = o * lax.rsqrt(jnp.mean(o * o, axis=0, keepdims=True) + EPS)
        o_ref[:, head * LANES:(head + 1) * LANES] = (o.T * subg_ref[...] * (1.0 - lam_init)).astype(BF16)


def _ctx_attention_call(kernel, name, q, k, vt, extra):
    tq = CTX_LEN
    g, dvx = vt.shape[1], vt.shape[3]
    in_specs = [
        pl.BlockSpec((tq, q.shape[1]), lambda b: (b, 0)),
        pl.BlockSpec((1, tq, k.shape[-1]), lambda b: (b, 0, 0)),
        pl.BlockSpec((1, g, 1, dvx, tq), lambda b: (b, 0, 0, 0, 0)),
    ] + [_resident(e.shape) for e in extra]
    return pl.pallas_call(
        kernel,
        grid=(BATCH,),
        in_specs=in_specs,
        out_specs=pl.BlockSpec((tq, q.shape[1]), lambda b: (b, 0)),
        out_shape=jax.ShapeDtypeStruct(q.shape, BF16),
        compiler_params=_cparams("parallel"),
        name=name,
    )(q, k, vt, *extra)


CONV_HALO = 16


def _conv_kernel(prev_ref, cur_ref, next_ref, w_ref, b_ref, g_ref, beta_ref, o_ref, win_ref, *, tc, nblk):
    j = pl.program_id(1)
    win_ref[0:CONV_HALO, :] = jnp.where(j > 0, prev_ref[0], 0.0)
    win_ref[CONV_HALO:CONV_HALO + tc, :] = cur_ref[0]
    win_ref[CONV_HALO + tc:CONV_HALO + tc + CONV_HALO, :] = jnp.where(j < nblk - 1, next_ref[0], 0.0)
    pad = B_CONV_W // 2
    acc = jnp.zeros((tc, B_CHANNELS), F32) + b_ref[...]
    for k in range(B_CONV_W):
        r0 = CONV_HALO - pad + k
        acc = acc + win_ref[r0:r0 + tc, :] * w_ref[k:k + 1, :]
    mu = jnp.mean(acc, axis=-1, keepdims=True)
    cen = acc - mu
    var = jnp.mean(cen * cen, axis=-1, keepdims=True)
    y = cen * lax.rsqrt(var + EPS) * g_ref[...] + beta_ref[...]
    o_ref[0] = _silu(y).astype(BF16)


def _conv_call(yb, dw_w, dw_b, ln_g, ln_b, *, tokens_per_batch):
    p = tokens_per_batch
    tc = 256
    nblk = p // tc
    r = tc // CONV_HALO
    nh = p // CONV_HALO
    y3 = yb.reshape(BATCH, p, B_CHANNELS)
    out = pl.pallas_call(
        functools.partial(_conv_kernel, tc=tc, nblk=nblk),
        grid=(BATCH, nblk),
        in_specs=[
            pl.BlockSpec((1, CONV_HALO, B_CHANNELS), lambda b, j: (b, jnp.maximum(j * r - 1, 0), 0)),
            pl.BlockSpec((1, tc, B_CHANNELS), lambda b, j: (b, j, 0)),
            pl.BlockSpec((1, CONV_HALO, B_CHANNELS), lambda b, j: (b, jnp.minimum((j + 1) * r, nh - 1), 0)),
            _resident((32, B_CHANNELS)),
            _resident((1, B_CHANNELS)),
            _resident((1, B_CHANNELS)),
            _resident((1, B_CHANNELS)),
        ],
        out_specs=pl.BlockSpec((1, tc, B_CHANNELS), lambda b, j: (b, j, 0)),
        out_shape=jax.ShapeDtypeStruct((BATCH, p, B_CHANNELS), BF16),
        scratch_shapes=[pltpu.VMEM((tc + 2 * CONV_HALO, B_CHANNELS), F32)],
        compiler_params=_cparams("parallel", "arbitrary"),
        name="conformer_conv",
    )(y3, y3, y3, dw_w, dw_b, ln_g, ln_b)
    return out.reshape(BATCH * p, B_CHANNELS)


def _merge_kernel(x_ref, mod_ref, oa_ref, ob_ref, oc_ref, gt_ref, wpa_ref, wpb_ref, wpc_ref, wout_ref, o_ref):
    d = D_MODEL
    m = (gt_ref[:, 0:d].astype(F32) * _dot(oa_ref[...], wpa_ref[...])
         + gt_ref[:, d:2 * d].astype(F32) * _dot(ob_ref[...], wpb_ref[...])
         + gt_ref[:, 2 * d:3 * d].astype(F32) * _dot(oc_ref[...], wpc_ref[...]))
    y = _dot(m.astype(BF16), wout_ref[...])
    gate1 = mod_ref[0][:, 2 * d:3 * d]
    o_ref[...] = x_ref[...] + gate1 * y


def _merge_call(x, mod3, mod_row, oa, ob, oc, gt, wpa, wpb, wpc, wout, *, tm):
    t = x.shape[0]
    row = lambda i: (i, 0)
    return pl.pallas_call(
        _merge_kernel,
        grid=(t // tm,),
        in_specs=[
            pl.BlockSpec((tm, D_MODEL), row),
            pl.BlockSpec((1, 1, 6 * D_MODEL), lambda i: (mod_row(i), 0, 0)),
            pl.BlockSpec((tm, A_Q), row),
            pl.BlockSpec((tm, B_CHANNELS), row),
            pl.BlockSpec((tm, C_V), row),
            pl.BlockSpec((tm, N_BRANCH * D_MODEL), row),
            _resident(wpa.shape), _resident(wpb.shape), _resident(wpc.shape), _resident(wout.shape),
        ],
        out_specs=pl.BlockSpec((tm, D_MODEL), row),
        out_shape=jax.ShapeDtypeStruct(x.shape, F32),
        compiler_params=_cparams("parallel"),
        name="branch_merge",
    )(x, mod3, oa, ob, oc, gt, wpa, wpb, wpc, wout)


FFN_SPLIT = 2


def _ffn_kernel(x_ref, mod_ref, g2_ref, w1_ref, w3_ref, w2_ref, o_ref):
    d = D_MODEL
    mod = mod_ref[0]
    x = x_ref[...]
    h = _norm_mod(x, g2_ref[...], mod[:, 3 * d:4 * d], mod[:, 4 * d:5 * d]).astype(BF16)
    fc = D_FF // FFN_SPLIT
    y = jnp.zeros(x.shape, F32)
    for s in range(FFN_SPLIT):
        a = _dot(h, w1_ref[:, s * fc:(s + 1) * fc])
        b = _dot(h, w3_ref[:, s * fc:(s + 1) * fc])
        y = y + _dot((_silu(a) * b).astype(BF16), w2_ref[s * fc:(s + 1) * fc, :])
    o_ref[...] = x + mod[:, 5 * d:6 * d] * y


def _ffn_call(x, mod3, mod_row, g2, w1, w3, w2, *, tm):
    t = x.shape[0]
    row = lambda i: (i, 0)
    return pl.pallas_call(
        _ffn_kernel,
        grid=(t // tm,),
        in_specs=[
            pl.BlockSpec((tm, D_MODEL), row),
            pl.BlockSpec((1, 1, 6 * D_MODEL), lambda i: (mod_row(i), 0, 0)),
            _resident((1, D_MODEL)),
            _resident(w1.shape), _resident(w3.shape), _resident(w2.shape),
        ],
        out_specs=pl.BlockSpec((tm, D_MODEL), row),
        out_shape=jax.ShapeDtypeStruct(x.shape, F32),
        compiler_params=_cparams("parallel"),
        name="dense_swiglu",
    )(x, mod3, g2, w1, w3, w2)


MOE_F_TILE = 1792


def _top2_gates(logits):
    lane = lax.broadcasted_iota(jnp.int32, logits.shape, 1)
    v1 = jnp.max(logits, axis=-1, keepdims=True)
    i1 = jnp.min(jnp.where(logits == v1, lane, LANES), axis=-1, keepdims=True)
    rest = jnp.where(lane == i1, -jnp.inf, logits)
    v2 = jnp.max(rest, axis=-1, keepdims=True)
    i2 = jnp.min(jnp.where(rest == v2, lane, LANES), axis=-1, keepdims=True)
    e2 = jnp.exp(v2 - v1)
    w1 = 1.0 / (1.0 + e2)
    return jnp.where(lane == i1, w1, 0.0) + jnp.where(lane == i2, e2 * w1, 0.0)


def _moe_kernel(x_ref, mod_ref, g2_ref, wr_ref, br_ref, w1_ref, w3_ref, w2_ref, fg_ref, o_ref,
                h_ref, gate_ref, acc_ref, *, final_norm):
    d = D_MODEL
    e = pl.program_id(1)
    f = pl.program_id(2)
    mod = mod_ref[0]

    @pl.when(jnp.logical_and(e == 0, f == 0))
    def _():
        h = _norm_mod(x_ref[...], g2_ref[...], mod[:, 3 * d:4 * d], mod[:, 4 * d:5 * d])
        logits = jnp.dot(h, wr_ref[...], preferred_element_type=F32, precision=lax.Precision.HIGHEST) + br_ref[...]
        gate_ref[...] = _top2_gates(logits)
        h_ref[...] = h.astype(BF16)
        acc_ref[...] = jnp.zeros_like(acc_ref)

    h = h_ref[...]
    z = (_silu(_dot(h, w1_ref[0])) * _dot(h, w3_ref[0])).astype(BF16)
    lane = lax.broadcasted_iota(jnp.int32, gate_ref.shape, 1)
    gcol = jnp.sum(jnp.where(lane == e, gate_ref[...], 0.0), axis=-1, keepdims=True)
    acc_ref[...] += gcol * _dot(z, w2_ref[0])

    @pl.when(jnp.logical_and(e == pl.num_programs(1) - 1, f == pl.num_programs(2) - 1))
    def _():
        y = x_ref[...] + mod[:, 5 * d:6 * d] * acc_ref[...]
        if final_norm:
            y = y * lax.rsqrt(jnp.mean(y * y, axis=-1, keepdims=True) + EPS) * fg_ref[...]
        o_ref[...] = y


def _moe_call(x, mod3, mod_row, g2, wr, br, w1, w3, w2, fg, *, tm, final_norm):
    t = x.shape[0]
    nf = D_FF_EXPERT // MOE_F_TILE
    row = lambda i, e, f: (i, 0)
    const2 = lambda i, e, f: (0, 0)
    return pl.pallas_call(
        functools.partial(_moe_kernel, final_norm=final_norm),
        grid=(t // tm, N_EXPERTS, nf),
        in_specs=[
            pl.BlockSpec((tm, D_MODEL), row),
            pl.BlockSpec((1, 1, 6 * D_MODEL), lambda i, e, f: (mod_row(i), 0, 0)),
            pl.BlockSpec((1, D_MODEL), const2),
            pl.BlockSpec((D_MODEL, LANES), const2),
            pl.BlockSpec((1, LANES), const2),
            pl.BlockSpec((1, D_MODEL, MOE_F_TILE), lambda i, e, f: (e, 0, f)),
            pl.BlockSpec((1, D_MODEL, MOE_F_TILE), lambda i, e, f: (e, 0, f)),
            pl.BlockSpec((1, MOE_F_TILE, D_MODEL), lambda i, e, f: (e, f, 0)),
            pl.BlockSpec((1, D_MODEL), const2),
        ],
        out_specs=pl.BlockSpec((tm, D_MODEL), row),
        out_shape=jax.ShapeDtypeStruct(x.shape, F32),
        scratch_shapes=[pltpu.VMEM((tm, D_MODEL), BF16), pltpu.VMEM((tm, LANES), F32),
                        pltpu.VMEM((tm, D_MODEL), F32)],
        compiler_params=_cparams("parallel", "arbitrary", "arbitrary"),
        name="moe_swiglu",
    )(x, mod3, g2, wr, br, w1, w3, w2, fg)


def _final_norm_kernel(x_ref, g_ref, o_ref):
    x = x_ref[...]
    o_ref[...] = x * lax.rsqrt(jnp.mean(x * x, axis=-1, keepdims=True) + EPS) * g_ref[...]


def _final_norm_call(x, g, *, tm):
    row = lambda i: (i, 0)
    return pl.pallas_call(
        _final_norm_kernel,
        grid=(x.shape[0] // tm,),
        in_specs=[pl.BlockSpec((tm, D_MODEL), row), _resident((1, D_MODEL))],
        out_specs=pl.BlockSpec((tm, D_MODEL), row),
        out_shape=jax.ShapeDtypeStruct(x.shape, F32),
        compiler_params=_cparams("parallel"),
        name="final_rmsnorm",
    )(x, g)


def _rope_tables():
    rows = SEQ // GRID_W
    row = jnp.repeat(jnp.arange(rows), GRID_W)
    col = jnp.tile(jnp.arange(GRID_W), rows)
    n_freq = HEAD_DIM // 4
    inv = ROPE_THETA ** (-jnp.arange(n_freq, dtype=F32) / n_freq)
    ang = jnp.stack([row, col], axis=-1).astype(F32)[:, :, None] * inv
    cos, sin = jnp.cos(ang), jnp.sin(ang)
    cos64 = jnp.concatenate([cos[:, 0], cos[:, 0], cos[:, 1], cos[:, 1]], axis=-1)
    sin64 = jnp.concatenate([-sin[:, 0], sin[:, 0], -sin[:, 1], sin[:, 1]], axis=-1)
    return jnp.tile(cos64, (1, 2)), jnp.tile(sin64, (1, 2))


def _pair_heads(w, axis):
    shp = w.shape
    w = w.reshape(shp[:axis] + (A_KV_HEADS, A_GROUP, HEAD_DIM) + shp[axis + 1:])
    w = jnp.swapaxes(w, axis, axis + 1)
    return w.reshape(shp)


def kernel(x, c, ctx, c_ctx, w_mod, b_mod, norm1_g, norm2_g, w_in, b_gate, a_qn_g, a_kn_g, b_dw_w, b_dw_b,
           b_ln_g, b_ln_b, c_lq1, c_lk1, c_lq2, c_lk2, c_subln_g, w_pa, w_pb, w_pc, w_out, ffn_w1, ffn_w3,
           ffn_w2, moe_router, moe_router_b, moe_w1, moe_w3, moe_w2, final_g):
    d = D_MODEL
    t_lat = BATCH * SEQ
    xl = x.reshape(t_lat, d)
    xc = ctx.reshape(BATCH * CTX_LEN, d)

    cc = jnp.concatenate([c, c_ctx[None, :], jnp.zeros((3, d), F32)], axis=0)
    mod_all = _mod_call(cc, w_mod, b_mod)

    cos_t, sin_t = _rope_tables()
    head_id = np.arange(A_Q) // HEAD_DIM
    bd = jnp.asarray(head_id[:, None] == head_id[None, :], BF16)

    lat_tiles_per_batch = SEQ // KEY_CHUNK
    lat_row = lambda i: i // lat_tiles_per_batch
    ctx_row = lambda i: 4
    tm = KEY_CHUNK

    for i in range(DEPTH):
        last = i == DEPTH - 1
        lam_init = 0.8 - 0.6 * math.exp(-0.3 * i)
        mod3 = mod_all[i].reshape(8, 1, 6 * d)
        g1 = norm1_g[i].reshape(1, d)
        g2 = norm2_g[i].reshape(1, d)
        w_in_i = jnp.concatenate([_pair_heads(w_in[i][:, :A_Q], 1), w_in[i][:, A_Q:]], axis=1).astype(BF16)
        bg = b_gate[i].reshape(1, -1)
        qn = jnp.tile(a_qn_g[i], A_HEADS).reshape(1, A_Q)
        kn = jnp.tile(a_kn_g[i], A_KV_HEADS).reshape(1, A_KV)
        lam_vecs = [v[i].reshape(1, HEAD_DIM) for v in (c_lq1, c_lk1, c_lq2, c_lk2)]
        subg = c_subln_g[i].reshape(1, C_V_DIM)
        wpa = _pair_heads(w_pa[i], 0).astype(BF16)
        wpb, wpc, wout = w_pb[i].astype(BF16), w_pc[i].astype(BF16), w_out[i].astype(BF16)
        dw_w = jnp.concatenate([b_dw_w[i], jnp.zeros((1, B_CHANNELS), F32)], axis=0)
        conv_p = (dw_w, b_dw_b[i].reshape(1, -1), b_ln_g[i].reshape(1, -1), b_ln_b[i].reshape(1, -1))

        proj = functools.partial(_inproj_call, g1=g1, w_in=w_in_i, b_gate=bg, qn=qn, kn=kn, bd=bd,
                                 cos_t=cos_t, sin_t=sin_t)
        qa, ka, vta, yb, qc, kc, vtc, gt = proj(xl, mod3, lat_row, tokens_per_batch=SEQ, rope=True)
        qa_c, ka_c, vta_c, yb_c, qc_c, kc_c, vtc_c, gt_c = proj(xc, mod3, ctx_row, tokens_per_batch=CTX_LEN,
                                                                rope=False)
        ka3, kc3 = ka.reshape(BATCH, SEQ, A_KV), kc.reshape(BATCH, SEQ, C_QK)
        ka3_c, kc3_c = ka_c.reshape(BATCH, CTX_LEN, A_KV), kc_c.reshape(BATCH, CTX_LEN, C_QK)

        oa = _attention_call(functools.partial(_gqa_kernel, tq=256), "gqa_attention", qa, ka3, ka3_c, vta, vta_c,
                             [], tokens_per_batch=SEQ, tq=256)
        oc = _attention_call(functools.partial(_diff_kernel, tq=256, lam_init=lam_init), "diff_attention",
                             qc, kc3, kc3_c, vtc, vtc_c, lam_vecs + [subg], tokens_per_batch=SEQ, tq=256)
        ob = _conv_call(yb, *conv_p, tokens_per_batch=SEQ)
        xl = _merge_call(xl, mod3, lat_row, oa, ob, oc, gt, wpa, wpb, wpc, wout, tm=tm)

        if not last:
            oa_c = _ctx_attention_call(functools.partial(_gqa_ctx_kernel, tq=CTX_LEN), "gqa_attention_ctx",
                                       qa_c, ka3_c, vta_c, [])
            oc_c = _ctx_attention_call(functools.partial(_diff_ctx_kernel, tq=CTX_LEN, lam_init=lam_init),
                                       "diff_attention_ctx", qc_c, kc3_c, vtc_c, lam_vecs + [subg])
            ob_c = _conv_call(yb_c, *conv_p, tokens_per_batch=CTX_LEN)
            xc = _merge_call(xc, mod3, ctx_row, oa_c, ob_c, oc_c, gt_c, wpa, wpb, wpc, wout, tm=tm)

        j = i // 2
        if i % 2 == 0:
            ffn_w = (ffn_w1[j].astype(BF16), ffn_w3[j].astype(BF16), ffn_w2[j].astype(BF16))
            xl = _ffn_call(xl, mod3, lat_row, g2, *ffn_w, tm=tm)
            if last:
                xl = _final_norm_call(xl, final_g.reshape(1, d), tm=tm)
            else:
                xc = _ffn_call(xc, mod3, ctx_row, g2, *ffn_w, tm=tm)
        else:
            wr = jnp.zeros((d, LANES), F32).at[:, :N_EXPERTS].set(moe_router[j])
            br = jnp.full((1, LANES), -jnp.inf, F32).at[0, :N_EXPERTS].set(moe_router_b[j])
            moe_w = (moe_w1[j].astype(BF16), moe_w3[j].astype(BF16), moe_w2[j].astype(BF16))
            fg = final_g.reshape(1, d)
            xl = _moe_call(xl, mod3, lat_row, g2, wr, br, *moe_w, fg, tm=tm, final_norm=last)
            if not last:
                xc = _moe_call(xc, mod3, ctx_row, g2, wr, br, *moe_w, fg, tm=tm, final_norm=False)
    return xl.reshape(BATCH, SEQ, d)
```

```python
import functools
import math

import numpy as np
import jax
import jax.numpy as jnp
from jax import lax
from jax.experimental import pallas as pl
from jax.experimental.pallas import tpu as pltpu

F32 = jnp.float32
BF16 = jnp.bfloat16

D_MODEL = 1024
BATCH = 4
SEQ = 4096
DEPTH = 2
GRID_W = 64
CTX_LEN = 256
EPS = 1e-6
ROPE_THETA = 10000.0

HEAD_DIM = 64
A_HEADS = 8
A_KV_HEADS = 2
A_GROUP = 4
A_Q = 512
A_KV = 128
B_CHANNELS = 512
B_CONV_W = 31
C_HEADS = 4
C_V_DIM = 128
C_QK = 512
C_V = 512
N_BRANCH = 3
IN_SPLITS = [A_Q, A_KV, A_KV, 2 * B_CHANNELS, C_QK, C_QK, C_V, N_BRANCH * D_MODEL]
IN_WIDTH = sum(IN_SPLITS)
IN_OFF = [0] + [int(v) for v in np.cumsum(IN_SPLITS)]
D_FF = 2816
N_EXPERTS = 8
D_FF_EXPERT = 3584

Q_SCALE = HEAD_DIM ** -0.5 * 1.4426950408889634

LANES = 128
ONES_ROWS = 16
KEY_CHUNK = 512
VMEM_LIMIT = 56 * 1024 * 1024


def _cparams(*sem):
    return pltpu.CompilerParams(dimension_semantics=sem, vmem_limit_bytes=VMEM_LIMIT)


def _resident(shape):
    nd = len(shape)
    return pl.BlockSpec(shape, lambda *_: (0,) * nd, pipeline_mode=pl.Buffered(1))


def _sigmoid(v):
    return 1.0 / (1.0 + jnp.exp(-v))


def _silu(v):
    return v * _sigmoid(v)


def _norm_mod(x, g, shift, scale):
    ms = jnp.mean(x * x, axis=-1, keepdims=True)
    return (x * lax.rsqrt(ms + EPS) * g) * (1.0 + scale) + shift


def _dot(a, b):
    return jnp.dot(a, b, preferred_element_type=F32)


def _mod_kernel(c_ref, w_ref, b_ref, o_ref):
    s = _silu(c_ref[...])
    o_ref[0] = jnp.dot(s, w_ref[0], preferred_element_type=F32, precision=lax.Precision.HIGHEST) + b_ref[0]


def _mod_call(cc, w_mod, b_mod):
    n = 6 * D_MODEL
    tn = D_MODEL
    return pl.pallas_call(
        _mod_kernel,
        grid=(DEPTH, n // tn),
        in_specs=[
            pl.BlockSpec((8, D_MODEL), lambda i, j: (0, 0)),
            pl.BlockSpec((1, D_MODEL, tn), lambda i, j: (i, 0, j)),
            pl.BlockSpec((1, 1, tn), lambda i, j: (i, 0, j)),
        ],
        out_specs=pl.BlockSpec((1, 8, tn), lambda i, j: (i, 0, j)),
        out_shape=jax.ShapeDtypeStruct((DEPTH, 8, n), F32),
        compiler_params=_cparams("arbitrary", "arbitrary"),
        name="adaln_mod",
    )(cc, w_mod, b_mod.reshape(DEPTH, 1, n))


def _segment_mean_sq(u, bd):
    sq = u * u
    hi = sq.astype(BF16)
    lo = (sq - hi.astype(F32)).astype(BF16)
    return (_dot(hi, bd) + _dot(lo, bd)) * (1.0 / HEAD_DIM)


def _rope_group(xg, cos, sin, lo_mask):
    partner = jnp.where(lo_mask, pltpu.roll(xg, LANES - 16, 1), pltpu.roll(xg, 16, 1))
    return xg * cos + partner * sin


def _inproj_kernel(x_ref, mod_ref, g1_ref, w_ref, bgate_ref, qn_ref, kn_ref, bd_ref, cos_ref, sin_ref,
                   qa_ref, ka_ref, vta_ref, yb_ref, qc_ref, kc_ref, vtc_ref, gt_ref, *, rope, tm):
    mod = mod_ref[0]
    h = _norm_mod(x_ref[...], g1_ref[...], mod[:, 0:D_MODEL], mod[:, D_MODEL:2 * D_MODEL]).astype(BF16)

    def proj(c0, c1):
        return _dot(h, w_ref[:, c0:c1])

    if rope:
        cos = cos_ref[...]
        sin = sin_ref[...]
        lane = lax.broadcasted_iota(jnp.int32, (tm, LANES), 1)
        lo_mask = (lane % 32) < 16

    def rot(u, j):
        ug = u[:, j * LANES:(j + 1) * LANES]
        return _rope_group(ug, cos, sin, lo_mask) if rope else ug

    ones_pat = jnp.where(lax.broadcasted_iota(jnp.int32, (ONES_ROWS, tm), 0) == 0, 1.0, 0.0).astype(BF16)

    u = proj(IN_OFF[0], IN_OFF[1])
    u = u * lax.rsqrt(_segment_mean_sq(u, bd_ref[...]) + EPS) * (qn_ref[...] * Q_SCALE)
    for j in range(A_Q // LANES):
        qa_ref[:, j * LANES:(j + 1) * LANES] = rot(u, j).astype(BF16)
    u = proj(IN_OFF[1], IN_OFF[2])
    u = u * lax.rsqrt(_segment_mean_sq(u, bd_ref[0:A_KV, 0:A_KV]) + EPS) * kn_ref[...]
    ka_ref[...] = rot(u, 0).astype(BF16)
    ut = proj(IN_OFF[2], IN_OFF[3]).T
    for j in range(A_KV_HEADS):
        vta_ref[0, j, 0, 0:HEAD_DIM, :] = ut[j * HEAD_DIM:(j + 1) * HEAD_DIM].astype(BF16)
        vta_ref[0, j, 0, HEAD_DIM:HEAD_DIM + ONES_ROWS, :] = ones_pat
    a = proj(IN_OFF[3], IN_OFF[3] + B_CHANNELS)
    g = proj(IN_OFF[3] + B_CHANNELS, IN_OFF[4])
    yb_ref[...] = a * _sigmoid(g)
    u = proj(IN_OFF[4], IN_OFF[5]) * Q_SCALE
    for j in range(C_QK // LANES):
        qc_ref[:, j * LANES:(j + 1) * LANES] = rot(u, j).astype(BF16)
    u = proj(IN_OFF[5], IN_OFF[6])
    for j in range(C_QK // LANES):
        kc_ref[:, j * LANES:(j + 1) * LANES] = rot(u, j).astype(BF16)
    ut = proj(IN_OFF[6], IN_OFF[7]).T
    for j in range(C_HEADS):
        vtc_ref[0, j, 0, 0:C_V_DIM, :] = ut[j * C_V_DIM:(j + 1) * C_V_DIM].astype(BF16)
        vtc_ref[0, j, 0, C_V_DIM:C_V_DIM + ONES_ROWS, :] = ones_pat
    for j in range(N_BRANCH):
        c0 = IN_OFF[7] + j * D_MODEL
        gt_ref[:, j * D_MODEL:(j + 1) * D_MODEL] = _sigmoid(
            proj(c0, c0 + D_MODEL) + bgate_ref[:, j * D_MODEL:(j + 1) * D_MODEL]).astype(BF16)


def _inproj_call(x, mod3, mod_row, g1, w_in, b_gate, qn, kn, bd, cos_t, sin_t, *, tokens_per_batch, rope):
    p = tokens_per_batch
    tm = min(p, KEY_CHUNK)
    nblk = p // tm
    t = BATCH * p
    row = lambda i: (i, 0)
    vt_map = lambda i: (i // nblk, 0, i % nblk, 0, 0)
    out_shape = (
        jax.ShapeDtypeStruct((t, A_Q), BF16),
        jax.ShapeDtypeStruct((t, A_KV), BF16),
        jax.ShapeDtypeStruct((BATCH, A_KV_HEADS, nblk, HEAD_DIM + ONES_ROWS, tm), BF16),
        jax.ShapeDtypeStruct((t, B_CHANNELS), F32),
        jax.ShapeDtypeStruct((t, C_QK), BF16),
        jax.ShapeDtypeStruct((t, C_QK), BF16),
        jax.ShapeDtypeStruct((BATCH, C_HEADS, nblk, C_V_DIM + ONES_ROWS, tm), BF16),
        jax.ShapeDtypeStruct((t, N_BRANCH * D_MODEL), BF16),
    )
    out_specs = (
        pl.BlockSpec((tm, A_Q), row),
        pl.BlockSpec((tm, A_KV), row),
        pl.BlockSpec((1, A_KV_HEADS, 1, HEAD_DIM + ONES_ROWS, tm), vt_map),
        pl.BlockSpec((tm, B_CHANNELS), row),
        pl.BlockSpec((tm, C_QK), row),
        pl.BlockSpec((tm, C_QK), row),
        pl.BlockSpec((1, C_HEADS, 1, C_V_DIM + ONES_ROWS, tm), vt_map),
        pl.BlockSpec((tm, N_BRANCH * D_MODEL), row),
    )
    in_specs = [
        pl.BlockSpec((tm, D_MODEL), row),
        pl.BlockSpec((1, 1, 6 * D_MODEL), lambda i: (mod_row(i), 0, 0)),
        _resident((1, D_MODEL)),
        _resident((D_MODEL, IN_WIDTH)),
        _resident((1, N_BRANCH * D_MODEL)),
        _resident((1, A_Q)),
        _resident((1, A_KV)),
        _resident((A_Q, A_Q)),
        pl.BlockSpec((tm, LANES), lambda i: (i % nblk, 0)),
        pl.BlockSpec((tm, LANES), lambda i: (i % nblk, 0)),
    ]
    return pl.pallas_call(
        functools.partial(_inproj_kernel, rope=rope, tm=tm),
        grid=(t // tm,),
        in_specs=in_specs,
        out_specs=out_specs,
        out_shape=out_shape,
        compiler_params=_cparams("parallel"),
        name="inproj_rope" if rope else "inproj_ctx",
    )(x, mod3, g1, w_in, b_gate, qn, kn, bd, cos_t, sin_t)


N_PROBLEMS = 8
STAGE_DELAYS = (3, 4, 5)


def _half_queries(q_ref, group, rows):
    qg = q_ref[:, group * LANES:(group + 1) * LANES]
    lane = lax.broadcasted_iota(jnp.int32, (rows, LANES), 1)
    zero = jnp.zeros_like(qg)
    return jnp.where(lane < HEAD_DIM, qg, zero), jnp.where(lane >= HEAD_DIM, qg, zero)


def _attend_all(q_ref, k_lat_ref, k_ctx_ref, vt_lat_ref, vt_ctx_ref, qx_ref, m_ref, acc_ref, k_lane0, vgroup, tq):
    for g in range(N_PROBLEMS // 2):
        qx_ref[2 * g], qx_ref[2 * g + 1] = _half_queries(q_ref, g, tq)
    m_ref[...] = jnp.full(m_ref.shape, -jnp.inf, F32)
    acc_ref[...] = jnp.zeros(acc_ref.shape, F32)
    dn = (((1,), (1,)), ((), ()))

    def step(load_k, load_vt):
        def scores(i):
            return lax.dot_general(load_k(k_lane0(i)), qx_ref[i], dn, preferred_element_type=F32)

        d_max, d_exp, d_pv = STAGE_DELAYS
        st, m_pair, p = {}, {}, {}
        for t in range(N_PROBLEMS + d_pv):
            if t < N_PROBLEMS:
                st[t] = scores(t)
            i = t - d_max
            if 0 <= i < N_PROBLEMS:
                m_old = m_ref[i]
                m_pair[i] = (m_old, jnp.maximum(m_old, jnp.max(st[i], axis=0, keepdims=True)))
            i = t - d_exp
            if 0 <= i < N_PROBLEMS:
                p[i] = jnp.exp2(st.pop(i) - m_pair[i][1]).astype(BF16)
            i = t - d_pv
            if 0 <= i < N_PROBLEMS:
                m_old, m_new = m_pair.pop(i)
                acc_ref[i] = jnp.exp2(m_old - m_new) * acc_ref[i] + _dot(load_vt(vgroup(i)), p.pop(i))
                m_ref[i] = m_new

    def body(c, carry):
        r0 = pl.multiple_of(c * KEY_CHUNK, KEY_CHUNK)
        step(lambda l0: k_lat_ref[0, pl.ds(r0, KEY_CHUNK), l0:l0 + LANES], lambda vg: vt_lat_ref[0, vg, c])
        return carry

    lax.fori_loop(0, k_lat_ref.shape[1] // KEY_CHUNK, body, 0)
    step(lambda l0: k_ctx_ref[0, :, l0:l0 + LANES], lambda vg: vt_ctx_ref[0, vg, 0])


def _gqa_kernel(q_ref, k_lat_ref, k_ctx_ref, vt_lat_ref, vt_ctx_ref, o_ref, qx_ref, m_ref, acc_ref, *, tq):
    _attend_all(q_ref, k_lat_ref, k_ctx_ref, vt_lat_ref, vt_ctx_ref, qx_ref, m_ref, acc_ref,
                lambda i: 0, lambda i: i % 2, tq)
    for pair in range(A_GROUP):
        outs = [acc_ref[2 * pair + h, 0:HEAD_DIM] / acc_ref[2 * pair + h, HEAD_DIM:HEAD_DIM + 1] for h in range(2)]
        o_ref[:, pair * LANES:(pair + 1) * LANES] = jnp.concatenate(outs, axis=0).T.astype(BF16)


def _diff_lambda(lq1_ref, lk1_ref, lq2_ref, lk2_ref, lam_init):
    return (jnp.exp(jnp.sum(lq1_ref[...] * lk1_ref[...], axis=-1, keepdims=True))
            - jnp.exp(jnp.sum(lq2_ref[...] * lk2_ref[...], axis=-1, keepdims=True)) + lam_init)


def _diff_combine(o1, o2, lam, subg, lam_init):
    o = o1 - lam * o2
    o = o * lax.rsqrt(jnp.mean(o * o, axis=0, keepdims=True) + EPS)
    return o.T * subg * (1.0 - lam_init)


def _diff_kernel(q_ref, k_lat_ref, k_ctx_ref, vt_lat_ref, vt_ctx_ref, lq1_ref, lk1_ref, lq2_ref, lk2_ref,
                 subg_ref, o_ref, qx_ref, m_ref, acc_ref, *, tq, lam_init):
    _attend_all(q_ref, k_lat_ref, k_ctx_ref, vt_lat_ref, vt_ctx_ref, qx_ref, m_ref, acc_ref,
                lambda i: (i // 2) * LANES, lambda i: i // 2, tq)
    lam = _diff_lambda(lq1_ref, lk1_ref, lq2_ref, lk2_ref, lam_init)
    for head in range(C_HEADS):
        outs = [acc_ref[2 * head + h, 0:C_V_DIM] / acc_ref[2 * head + h, C_V_DIM:C_V_DIM + 1] for h in range(2)]
        o_ref[:, head * LANES:(head + 1) * LANES] = _diff_combine(
            outs[0], outs[1], lam, subg_ref[...], lam_init).astype(BF16)


def _attention_call(kernel, name, q, k_lat, k_ctx, vt_lat, vt_ctx, extra, *, tokens_per_batch, tq):
    nq = tokens_per_batch // tq
    kw = k_lat.shape[-1]
    g, dvx = vt_lat.shape[1], vt_lat.shape[3]
    bmap3 = lambda b, i: (b, 0, 0)
    bmap5 = lambda b, i: (b, 0, 0, 0, 0)
    in_specs = [
        pl.BlockSpec((tq, q.shape[1]), lambda b, i: (b * nq + i, 0)),
        pl.BlockSpec((1, k_lat.shape[1], kw), bmap3),
        pl.BlockSpec((1, k_ctx.shape[1], kw), bmap3),
        pl.BlockSpec((1, g, vt_lat.shape[2], dvx, vt_lat.shape[4]), bmap5),
        pl.BlockSpec((1, g, 1, dvx, vt_ctx.shape[4]), bmap5),
    ] + [_resident(e.shape) for e in extra]
    return pl.pallas_call(
        kernel,
        grid=(BATCH, nq),
        in_specs=in_specs,
        out_specs=pl.BlockSpec((tq, q.shape[1]), lambda b, i: (b * nq + i, 0)),
        out_shape=jax.ShapeDtypeStruct(q.shape, BF16),
        scratch_shapes=[pltpu.VMEM((N_PROBLEMS, tq, LANES), BF16), pltpu.VMEM((N_PROBLEMS, 1, tq), F32),
                        pltpu.VMEM((N_PROBLEMS, dvx, tq), F32)],
        compiler_params=_cparams("parallel", "arbitrary"),
        name=name,
    )(q, k_lat, k_ctx, vt_lat, vt_ctx, *extra)


def _attend_ctx(q_ext, k, vt):
    dn = (((1,), (1,)), ((), ()))
    st = lax.dot_general(k, q_ext, dn, preferred_element_type=F32)
    p = jnp.exp2(st - jnp.max(st, axis=0, keepdims=True)).astype(BF16)
    return _dot(vt, p)


def _gqa_ctx_kernel(q_ref, k_ref, vt_ref, o_ref, *, tq):
    for pair in range(A_GROUP):
        outs = []
        for half, q_ext in enumerate(_half_queries(q_ref, pair, tq)):
            acc = _attend_ctx(q_ext, k_ref[0], vt_ref[0, half, 0])
            outs.append(acc[0:HEAD_DIM] / acc[HEAD_DIM:HEAD_DIM + 1])
        o_ref[:, pair * LANES:(pair + 1) * LANES] = jnp.concatenate(outs, axis=0).T.astype(BF16)


def _diff_ctx_kernel(q_ref, k_ref, vt_ref, lq1_ref, lk1_ref, lq2_ref, lk2_ref, subg_ref, o_ref, *, tq, lam_init):
    lam = _diff_lambda(lq1_ref, lk1_ref, lq2_ref, lk2_ref, lam_init)
    for head in range(C_HEADS):
        outs = []
        for q_ext in _half_queries(q_ref, head, tq):
            acc = _attend_ctx(q_ext, k_ref[0, :, head * LANES:(head + 1) * LANES], vt_ref[0, head, 0])
            outs.append(acc[0:C_V_DIM] / acc[C_V_DIM:C_V_DIM + 1])
        o_ref[:, head * LANES:(head + 1) * LANES] = _diff_combine(
            outs[0], outs[1], lam, subg_ref[...], lam_init).astype(BF16)


def _ctx_attention_call(kernel, name, q, k, vt, extra):
    tq = CTX_LEN
    g, dvx = vt.shape[1], vt.shape[3]
    in_specs = [
        pl.BlockSpec((tq, q.shape[1]), lambda b: (b, 0)),
        pl.BlockSpec((1, tq, k.shape[-1]), lambda b: (b, 0, 0)),
        pl.BlockSpec((1, g, 1, dvx, tq), lambda b: (b, 0, 0, 0, 0)),
    ] + [_resident(e.shape) for e in extra]
    return pl.pallas_call(
        kernel,
        grid=(BATCH,),
        in_specs=in_specs,
        out_specs=pl.BlockSpec((tq, q.shape[1]), lambda b: (b, 0)),
        out_shape=jax.ShapeDtypeStruct(q.shape, BF16),
        compiler_params=_cparams("parallel"),
        name=name,
    )(q, k, vt, *extra)


CONV_HALO = 16


def _conv_kernel(prev_ref, cur_ref, next_ref, w_ref, b_ref, g_ref, beta_ref, o_ref, win_ref, *, tc, nblk):
    j = pl.program_id(1)
    win_ref[0:CONV_HALO, :] = jnp.where(j > 0, prev_ref[0], 0.0)
    win_ref[CONV_HALO:CONV_HALO + tc, :] = cur_ref[0]
    win_ref[CONV_HALO + tc:CONV_HALO + tc + CONV_HALO, :] = jnp.where(j < nblk - 1, next_ref[0], 0.0)
    pad = B_CONV_W // 2
    acc = jnp.zeros((tc, B_CHANNELS), F32) + b_ref[...]
    for k in range(B_CONV_W):
        r0 = CONV_HALO - pad + k
        acc = acc + win_ref[r0:r0 + tc, :] * w_ref[k:k + 1, :]
    mu = jnp.mean(acc, axis=-1, keepdims=True)
    cen = acc - mu
    var = jnp.mean(cen * cen, axis=-1, keepdims=True)
    y = cen * lax.rsqrt(var + EPS) * g_ref[...] + beta_ref[...]
    o_ref[0] = _silu(y).astype(BF16)


def _conv_call(yb, dw_w, dw_b, ln_g, ln_b, *, tokens_per_batch):
    p = tokens_per_batch
    tc = 256
    nblk = p // tc
    r = tc // CONV_HALO
    nh = p // CONV_HALO
    y3 = yb.reshape(BATCH, p, B_CHANNELS)
    out = pl.pallas_call(
        functools.partial(_conv_kernel, tc=tc, nblk=nblk),
        grid=(BATCH, nblk),
        in_specs=[
            pl.BlockSpec((1, CONV_HALO, B_CHANNELS), lambda b, j: (b, jnp.maximum(j * r - 1, 0), 0)),
            pl.BlockSpec((1, tc, B_CHANNELS), lambda b, j: (b, j, 0)),
            pl.BlockSpec((1, CONV_HALO, B_CHANNELS), lambda b, j: (b, jnp.minimum((j + 1) * r, nh - 1), 0)),
            _resident((32, B_CHANNELS)),
            _resident((1, B_CHANNELS)),
            _resident((1, B_CHANNELS)),
            _resident((1, B_CHANNELS)),
        ],
        out_specs=pl.BlockSpec((1, tc, B_CHANNELS), lambda b, j: (b, j, 0)),
        out_shape=jax.ShapeDtypeStruct((BATCH, p, B_CHANNELS), BF16),
        scratch_shapes=[pltpu.VMEM((tc + 2 * CONV_HALO, B_CHANNELS), F32)],
        compiler_params=_cparams("parallel", "arbitrary"),
        name="conformer_conv",
    )(y3, y3, y3, dw_w, dw_b, ln_g, ln_b)
    return out.reshape(BATCH * p, B_CHANNELS)


def _merge_kernel(x_ref, mod_ref, oa_ref, ob_ref, oc_ref, gt_ref, wpa_ref, wpb_ref, wpc_ref, wout_ref, o_ref):
    d = D_MODEL
    m = (gt_ref[:, 0:d].astype(F32) * _dot(oa_ref[...], wpa_ref[...])
         + gt_ref[:, d:2 * d].astype(F32) * _dot(ob_ref[...], wpb_ref[...])
         + gt_ref[:, 2 * d:3 * d].astype(F32) * _dot(oc_ref[...], wpc_ref[...]))
    y = _dot(m.astype(BF16), wout_ref[...])
    gate1 = mod_ref[0][:, 2 * d:3 * d]
    o_ref[...] = x_ref[...] + gate1 * y


def _merge_call(x, mod3, mod_row, oa, ob, oc, gt, wpa, wpb, wpc, wout, *, tm):
    t = x.shape[0]
    row = lambda i: (i, 0)
    return pl.pallas_call(
        _merge_kernel,
        grid=(t // tm,),
        in_specs=[
            pl.BlockSpec((tm, D_MODEL), row),
            pl.BlockSpec((1, 1, 6 * D_MODEL), lambda i: (mod_row(i), 0, 0)),
            pl.BlockSpec((tm, A_Q), row),
            pl.BlockSpec((tm, B_CHANNELS), row),
            pl.BlockSpec((tm, C_V), row),
            pl.BlockSpec((tm, N_BRANCH * D_MODEL), row),
            _resident(wpa.shape), _resident(wpb.shape), _resident(wpc.shape), _resident(wout.shape),
        ],
        out_specs=pl.BlockSpec((tm, D_MODEL), row),
        out_shape=jax.ShapeDtypeStruct(x.shape, F32),
        compiler_params=_cparams("parallel"),
        name="branch_merge",
    )(x, mod3, oa, ob, oc, gt, wpa, wpb, wpc, wout)


FFN_SPLIT = 2


def _ffn_kernel(x_ref, mod_ref, g2_ref, w1_ref, w3_ref, w2_ref, o_ref):
    d = D_MODEL
    mod = mod_ref[0]
    x = x_ref[...]
    h = _norm_mod(x, g2_ref[...], mod[:, 3 * d:4 * d], mod[:, 4 * d:5 * d]).astype(BF16)
    fc = D_FF // FFN_SPLIT
    y = jnp.zeros(x.shape, F32)
    for s in range(FFN_SPLIT):
        a = _dot(h, w1_ref[:, s * fc:(s + 1) * fc])
        b = _dot(h, w3_ref[:, s * fc:(s + 1) * fc])
        y = y + _dot((_silu(a) * b).astype(BF16), w2_ref[s * fc:(s + 1) * fc, :])
    o_ref[...] = x + mod[:, 5 * d:6 * d] * y


def _ffn_call(x, mod3, mod_row, g2, w1, w3, w2, *, tm):
    t = x.shape[0]
    row = lambda i: (i, 0)
    return pl.pallas_call(
        _ffn_kernel,
        grid=(t // tm,),
        in_specs=[
            pl.BlockSpec((tm, D_MODEL), row),
            pl.BlockSpec((1, 1, 6 * D_MODEL), lambda i: (mod_row(i), 0, 0)),
            _resident((1, D_MODEL)),
            _resident(w1.shape), _resident(w3.shape), _resident(w2.shape),
        ],
        out_specs=pl.BlockSpec((tm, D_MODEL), row),
        out_shape=jax.ShapeDtypeStruct(x.shape, F32),
        compiler_params=_cparams("parallel"),
        name="dense_swiglu",
    )(x, mod3, g2, w1, w3, w2)


MOE_F_TILE = 1792


def _top2_gates(logits):
    lane = lax.broadcasted_iota(jnp.int32, logits.shape, 1)
    v1 = jnp.max(logits, axis=-1, keepdims=True)
    i1 = jnp.min(jnp.where(logits == v1, lane, LANES), axis=-1, keepdims=True)
    rest = jnp.where(lane == i1, -jnp.inf, logits)
    v2 = jnp.max(rest, axis=-1, keepdims=True)
    i2 = jnp.min(jnp.where(rest == v2, lane, LANES), axis=-1, keepdims=True)
    e2 = jnp.exp(v2 - v1)
    w1 = 1.0 / (1.0 + e2)
    return jnp.where(lane == i1, w1, 0.0) + jnp.where(lane == i2, e2 * w1, 0.0)


def _moe_kernel(x_ref, mod_ref, g2_ref, wr_ref, br_ref, w1_ref, w3_ref, w2_ref, fg_ref, o_ref,
                h_ref, gate_ref, acc_ref, *, final_norm):
    d = D_MODEL
    e = pl.program_id(1)
    f = pl.program_id(2)
    mod = mod_ref[0]

    @pl.when(jnp.logical_and(e == 0, f == 0))
    def _():
        h = _norm_mod(x_ref[...], g2_ref[...], mod[:, 3 * d:4 * d], mod[:, 4 * d:5 * d])
        logits = jnp.dot(h, wr_ref[...], preferred_element_type=F32, precision=lax.Precision.HIGHEST) + br_ref[...]
        gate_ref[...] = _top2_gates(logits)
        h_ref[...] = h.astype(BF16)
        acc_ref[...] = jnp.zeros_like(acc_ref)

    h = h_ref[...]
    z = (_silu(_dot(h, w1_ref[0])) * _dot(h, w3_ref[0])).astype(BF16)
    lane = lax.broadcasted_iota(jnp.int32, gate_ref.shape, 1)
    gcol = jnp.sum(jnp.where(lane == e, gate_ref[...], 0.0), axis=-1, keepdims=True)
    acc_ref[...] += gcol * _dot(z, w2_ref[0])

    @pl.when(jnp.logical_and(e == pl.num_programs(1) - 1, f == pl.num_programs(2) - 1))
    def _():
        y = x_ref[...] + mod[:, 5 * d:6 * d] * acc_ref[...]
        if final_norm:
            y = y * lax.rsqrt(jnp.mean(y * y, axis=-1, keepdims=True) + EPS) * fg_ref[...]
        o_ref[...] = y


def _moe_call(x, mod3, mod_row, g2, wr, br, w1, w3, w2, fg, *, tm, final_norm):
    t = x.shape[0]
    nf = D_FF_EXPERT // MOE_F_TILE
    row = lambda i, e, f: (i, 0)
    const2 = lambda i, e, f: (0, 0)
    return pl.pallas_call(
        functools.partial(_moe_kernel, final_norm=final_norm),
        grid=(t // tm, N_EXPERTS, nf),
        in_specs=[
            pl.BlockSpec((tm, D_MODEL), row),
            pl.BlockSpec((1, 1, 6 * D_MODEL), lambda i, e, f: (mod_row(i), 0, 0)),
            pl.BlockSpec((1, D_MODEL), const2),
            pl.BlockSpec((D_MODEL, LANES), const2),
            pl.BlockSpec((1, LANES), const2),
            pl.BlockSpec((1, D_MODEL, MOE_F_TILE), lambda i, e, f: (e, 0, f)),
            pl.BlockSpec((1, D_MODEL, MOE_F_TILE), lambda i, e, f: (e, 0, f)),
            pl.BlockSpec((1, MOE_F_TILE, D_MODEL), lambda i, e, f: (e, f, 0)),
            pl.BlockSpec((1, D_MODEL), const2),
        ],
        out_specs=pl.BlockSpec((tm, D_MODEL), row),
        out_shape=jax.ShapeDtypeStruct(x.shape, F32),
        scratch_shapes=[pltpu.VMEM((tm, D_MODEL), BF16), pltpu.VMEM((tm, LANES), F32),
                        pltpu.VMEM((tm, D_MODEL), F32)],
        compiler_params=_cparams("parallel", "arbitrary", "arbitrary"),
        name="moe_swiglu",
    )(x, mod3, g2, wr, br, w1, w3, w2, fg)


def _final_norm_kernel(x_ref, g_ref, o_ref):
    x = x_ref[...]
    o_ref[...] = x * lax.rsqrt(jnp.mean(x * x, axis=-1, keepdims=True) + EPS) * g_ref[...]


def _final_norm_call(x, g, *, tm):
    row = lambda i: (i, 0)
    return pl.pallas_call(
        _final_norm_kernel,
        grid=(x.shape[0] // tm,),
        in_specs=[pl.BlockSpec((tm, D_MODEL), row), _resident((1, D_MODEL))],
        out_specs=pl.BlockSpec((tm, D_MODEL), row),
        out_shape=jax.ShapeDtypeStruct(x.shape, F32),
        compiler_params=_cparams("parallel"),
        name="final_rmsnorm",
    )(x, g)


def _rope_tables():
    rows = SEQ // GRID_W
    row = jnp.repeat(jnp.arange(rows), GRID_W)
    col = jnp.tile(jnp.arange(GRID_W), rows)
    n_freq = HEAD_DIM // 4
    inv = ROPE_THETA ** (-jnp.arange(n_freq, dtype=F32) / n_freq)
    ang = jnp.stack([row, col], axis=-1).astype(F32)[:, :, None] * inv
    cos, sin = jnp.cos(ang), jnp.sin(ang)
    cos64 = jnp.concatenate([cos[:, 0], cos[:, 0], cos[:, 1], cos[:, 1]], axis=-1)
    sin64 = jnp.concatenate([-sin[:, 0], sin[:, 0], -sin[:, 1], sin[:, 1]], axis=-1)
    return jnp.tile(cos64, (1, 2)), jnp.tile(sin64, (1, 2))


def _pair_heads(w, axis):
    shp = w.shape
    w = w.reshape(shp[:axis] + (A_KV_HEADS, A_GROUP, HEAD_DIM) + shp[axis + 1:])
    w = jnp.swapaxes(w, axis, axis + 1)
    return w.reshape(shp)


def kernel(x, c, ctx, c_ctx, w_mod, b_mod, norm1_g, norm2_g, w_in, b_gate, a_qn_g, a_kn_g, b_dw_w, b_dw_b,
           b_ln_g, b_ln_b, c_lq1, c_lk1, c_lq2, c_lk2, c_subln_g, w_pa, w_pb, w_pc, w_out, ffn_w1, ffn_w3,
           ffn_w2, moe_router, moe_router_b, moe_w1, moe_w3, moe_w2, final_g):
    d = D_MODEL
    t_lat = BATCH * SEQ
    xl = x.reshape(t_lat, d)
    xc = ctx.reshape(BATCH * CTX_LEN, d)

    cc = jnp.concatenate([c, c_ctx[None, :], jnp.zeros((3, d), F32)], axis=0)
    mod_all = _mod_call(cc, w_mod, b_mod)

    cos_t, sin_t = _rope_tables()
    head_id = np.arange(A_Q) // HEAD_DIM
    bd = jnp.asarray(head_id[:, None] == head_id[None, :], BF16)

    lat_tiles_per_batch = SEQ // KEY_CHUNK
    lat_row = lambda i: i // lat_tiles_per_batch
    ctx_row = lambda i: 4
    tm = KEY_CHUNK

    for i in range(DEPTH):
        last = i == DEPTH - 1
        lam_init = 0.8 - 0.6 * math.exp(-0.3 * i)
        mod3 = mod_all[i].reshape(8, 1, 6 * d)
        g1 = norm1_g[i].reshape(1, d)
        g2 = norm2_g[i].reshape(1, d)
        w_in_i = jnp.concatenate([_pair_heads(w_in[i][:, :A_Q], 1), w_in[i][:, A_Q:]], axis=1).astype(BF16)
        bg = b_gate[i].reshape(1, -1)
        qn = jnp.tile(a_qn_g[i], A_HEADS).reshape(1, A_Q)
        kn = jnp.tile(a_kn_g[i], A_KV_HEADS).reshape(1, A_KV)
        lam_vecs = [v[i].reshape(1, HEAD_DIM) for v in (c_lq1, c_lk1, c_lq2, c_lk2)]
        subg = c_subln_g[i].reshape(1, C_V_DIM)
        wpa = _pair_heads(w_pa[i], 0).astype(BF16)
        wpb, wpc, wout = w_pb[i].astype(BF16), w_pc[i].astype(BF16), w_out[i].astype(BF16)
        dw_w = jnp.concatenate([b_dw_w[i], jnp.zeros((1, B_CHANNELS), F32)], axis=0)
        conv_p = (dw_w, b_dw_b[i].reshape(1, -1), b_ln_g[i].reshape(1, -1), b_ln_b[i].reshape(1, -1))

        proj = functools.partial(_inproj_call, g1=g1, w_in=w_in_i, b_gate=bg, qn=qn, kn=kn, bd=bd,
                                 cos_t=cos_t, sin_t=sin_t)
        qa, ka, vta, yb, qc, kc, vtc, gt = proj(xl, mod3, lat_row, tokens_per_batch=SEQ, rope=True)
        qa_c, ka_c, vta_c, yb_c, qc_c, kc_c, vtc_c, gt_c = proj(xc, mod3, ctx_row, tokens_per_batch=CTX_LEN,
                                                                rope=False)
        ka3, kc3 = ka.reshape(BATCH, SEQ, A_KV), kc.reshape(BATCH, SEQ, C_QK)
        ka3_c, kc3_c = ka_c.reshape(BATCH, CTX_LEN, A_KV), kc_c.reshape(BATCH, CTX_LEN, C_QK)

        oa = _attention_call(functools.partial(_gqa_kernel, tq=256), "gqa_attention", qa, ka3, ka3_c, vta, vta_c,
                             [], tokens_per_batch=SEQ, tq=256)
        oc = _attention_call(functools.partial(_diff_kernel, tq=256, lam_init=lam_init), "diff_attention",
                             qc, kc3, kc3_c, vtc, vtc_c, lam_vecs + [subg], tokens_per_batch=SEQ, tq=256)
        ob = _conv_call(yb, *conv_p, tokens_per_batch=SEQ)
        xl = _merge_call(xl, mod3, lat_row, oa, ob, oc, gt, wpa, wpb, wpc, wout, tm=tm)

        if not last:
            oa_c = _ctx_attention_call(functools.partial(_gqa_ctx_kernel, tq=CTX_LEN), "gqa_attention_ctx",
                                       qa_c, ka3_c, vta_c, [])
            oc_c = _ctx_attention_call(functools.partial(_diff_ctx_kernel, tq=CTX_LEN, lam_init=lam_init),
                                       "diff_attention_ctx", qc_c, kc3_c, vtc_c, lam_vecs + [subg])
            ob_c = _conv_call(yb_c, *conv_p, tokens_per_batch=CTX_LEN)
            xc = _merge_call(xc, mod3, ctx_row, oa_c, ob_c, oc_c, gt_c, wpa, wpb, wpc, wout, tm=tm)

        j = i // 2
        if i % 2 == 0:
            ffn_w = (ffn_w1[j].astype(BF16), ffn_w3[j].astype(BF16), ffn_w2[j].astype(BF16))
            xl = _ffn_call(xl, mod3, lat_row, g2, *ffn_w, tm=tm)
            if last:
                xl = _final_norm_call(xl, final_g.reshape(1, d), tm=tm)
            else:
                xc = _ffn_call(xc, mod3, ctx_row, g2, *ffn_w, tm=tm)
        else:
            wr = jnp.zeros((d, LANES), F32).at[:, :N_EXPERTS].set(moe_router[j])
            br = jnp.full((1, LANES), -jnp.inf, F32).at[0, :N_EXPERTS].set(moe_router_b[j])
            moe_w = (moe_w1[j].astype(BF16), moe_w3[j].astype(BF16), moe_w2[j].astype(BF16))
            fg = final_g.reshape(1, d)
            xl = _moe_call(xl, mod3, lat_row, g2, wr, br, *moe_w, fg, tm=tm, final_norm=last)
            if not last:
                xc = _moe_call(xc, mod3, ctx_row, g2, wr, br, *moe_w, fg, tm=tm, final_norm=False)
    return xl.reshape(BATCH, SEQ, d)
```

```python
import functools
import math

import numpy as np
import jax
import jax.numpy as jnp
from jax import lax
from jax.experimental import pallas as pl
from jax.experimental.pallas import tpu as pltpu

F32 = jnp.float32
BF16 = jnp.bfloat16

D_MODEL = 1024
BATCH = 4
SEQ = 4096
DEPTH = 2
GRID_W = 64
CTX_LEN = 256
EPS = 1e-6
ROPE_THETA = 10000.0

HEAD_DIM = 64
A_HEADS = 8
A_KV_HEADS = 2
A_GROUP = 4
A_Q = 512
A_KV = 128
B_CHANNELS = 512
B_CONV_W = 31
C_HEADS = 4
C_V_DIM = 128
C_QK = 512
C_V = 512
N_BRANCH = 3
IN_SPLITS = [A_Q, A_KV, A_KV, 2 * B_CHANNELS, C_QK, C_QK, C_V, N_BRANCH * D_MODEL]
IN_WIDTH = sum(IN_SPLITS)
IN_OFF = [0] + [int(v) for v in np.cumsum(IN_SPLITS)]
D_FF = 2816
N_EXPERTS = 8
D_FF_EXPERT = 3584

Q_SCALE = HEAD_DIM ** -0.5 * 1.4426950408889634

LANES = 128
ONES_ROWS = 16
KEY_CHUNK = 512
VMEM_LIMIT = 56 * 1024 * 1024


def _cparams(*sem):
    return pltpu.CompilerParams(dimension_semantics=sem, vmem_limit_bytes=VMEM_LIMIT)


def _resident(shape):
    nd = len(shape)
    return pl.BlockSpec(shape, lambda *_: (0,) * nd, pipeline_mode=pl.Buffered(1))


def _sigmoid(v):
    return 1.0 / (1.0 + jnp.exp(-v))


def _silu(v):
    return v * _sigmoid(v)


def _norm_mod(x, g, shift, scale):
    ms = jnp.mean(x * x, axis=-1, keepdims=True)
    return (x * lax.rsqrt(ms + EPS) * g) * (1.0 + scale) + shift


def _dot(a, b):
    return jnp.dot(a, b, preferred_element_type=F32)


def _mod_kernel(c_ref, w_ref, b_ref, o_ref):
    s = _silu(c_ref[...])
    o_ref[0] = jnp.dot(s, w_ref[0], preferred_element_type=F32, precision=lax.Precision.HIGHEST) + b_ref[0]


def _mod_call(cc, w_mod, b_mod):
    n = 6 * D_MODEL
    tn = D_MODEL
    return pl.pallas_call(
        _mod_kernel,
        grid=(DEPTH, n // tn),
        in_specs=[
            pl.BlockSpec((8, D_MODEL), lambda i, j: (0, 0)),
            pl.BlockSpec((1, D_MODEL, tn), lambda i, j: (i, 0, j)),
            pl.BlockSpec((1, 1, tn), lambda i, j: (i, 0, j)),
        ],
        out_specs=pl.BlockSpec((1, 8, tn), lambda i, j: (i, 0, j)),
        out_shape=jax.ShapeDtypeStruct((DEPTH, 8, n), F32),
        compiler_params=_cparams("arbitrary", "arbitrary"),
        name="adaln_mod",
    )(cc, w_mod, b_mod.reshape(DEPTH, 1, n))


def _segment_mean_sq(u, bd):
    sq = u * u
    hi = sq.astype(BF16)
    lo = (sq - hi.astype(F32)).astype(BF16)
    return (_dot(hi, bd) + _dot(lo, bd)) * (1.0 / HEAD_DIM)


def _rope_group(xg, cos, sin, lo_mask):
    partner = jnp.where(lo_mask, pltpu.roll(xg, LANES - 16, 1), pltpu.roll(xg, 16, 1))
    return xg * cos + partner * sin


def _inproj_kernel(x_ref, mod_ref, g1_ref, w_ref, bgate_ref, qn_ref, kn_ref, bd_ref, cos_ref, sin_ref,
                   qa_ref, ka_ref, vta_ref, yb_ref, qc_ref, kc_ref, vtc_ref, gt_ref, *, rope, tm):
    mod = mod_ref[0]
    h = _norm_mod(x_ref[...], g1_ref[...], mod[:, 0:D_MODEL], mod[:, D_MODEL:2 * D_MODEL]).astype(BF16)

    def proj(c0, c1):
        return _dot(h, w_ref[:, c0:c1])

    if rope:
        cos = cos_ref[...]
        sin = sin_ref[...]
        lane = lax.broadcasted_iota(jnp.int32, (tm, LANES), 1)
        lo_mask = (lane % 32) < 16

    def rot(u, j):
        ug = u[:, j * LANES:(j + 1) * LANES]
        return _rope_group(ug, cos, sin, lo_mask) if rope else ug

    ones_pat = jnp.where(lax.broadcasted_iota(jnp.int32, (ONES_ROWS, tm), 0) == 0, 1.0, 0.0).astype(BF16)

    u = proj(IN_OFF[0], IN_OFF[1])
    u = u * lax.rsqrt(_segment_mean_sq(u, bd_ref[...]) + EPS) * (qn_ref[...] * Q_SCALE)
    for j in range(A_Q // LANES):
        qa_ref[:, j * LANES:(j + 1) * LANES] = rot(u, j).astype(BF16)
    u = proj(IN_OFF[1], IN_OFF[2])
    u = u * lax.rsqrt(_segment_mean_sq(u, bd_ref[0:A_KV, 0:A_KV]) + EPS) * kn_ref[...]
    ka_ref[...] = rot(u, 0).astype(BF16)
    ut = proj(IN_OFF[2], IN_OFF[3]).T
    for j in range(A_KV_HEADS):
        vta_ref[0, j, 0, 0:HEAD_DIM, :] = ut[j * HEAD_DIM:(j + 1) * HEAD_DIM].astype(BF16)
        vta_ref[0, j, 0, HEAD_DIM:HEAD_DIM + ONES_ROWS, :] = ones_pat
    a = proj(IN_OFF[3], IN_OFF[3] + B_CHANNELS)
    g = proj(IN_OFF[3] + B_CHANNELS, IN_OFF[4])
    yb_ref[...] = a * _sigmoid(g)
    u = proj(IN_OFF[4], IN_OFF[5]) * Q_SCALE
    for j in range(C_QK // LANES):
        qc_ref[:, j * LANES:(j + 1) * LANES] = rot(u, j).astype(BF16)
    u = proj(IN_OFF[5], IN_OFF[6])
    for j in range(C_QK // LANES):
        kc_ref[:, j * LANES:(j + 1) * LANES] = rot(u, j).astype(BF16)
    ut = proj(IN_OFF[6], IN_OFF[7]).T
    for j in range(C_HEADS):
        vtc_ref[0, j, 0, 0:C_V_DIM, :] = ut[j * C_V_DIM:(j + 1) * C_V_DIM].astype(BF16)
        vtc_ref[0, j, 0, C_V_DIM:C_V_DIM + ONES_ROWS, :] = ones_pat
    for j in range(N_BRANCH):
        c0 = IN_OFF[7] + j * D_MODEL
        gt_ref[:, j * D_MODEL:(j + 1) * D_MODEL] = _sigmoid(
            proj(c0, c0 + D_MODEL) + bgate_ref[:, j * D_MODEL:(j + 1) * D_MODEL]).astype(BF16)


def _inproj_call(x, mod3, mod_row, g1, w_in, b_gate, qn, kn, bd, cos_t, sin_t, *, tokens_per_batch, rope):
    p = tokens_per_batch
    tm = min(p, KEY_CHUNK)
    nblk = p // tm
    t = BATCH * p
    row = lambda i: (i, 0)
    vt_map = lambda i: (i // nblk, 0, i % nblk, 0, 0)
    out_shape = (
        jax.ShapeDtypeStruct((t, A_Q), BF16),
        jax.ShapeDtypeStruct((t, A_KV), BF16),
        jax.ShapeDtypeStruct((BATCH, A_KV_HEADS, nblk, HEAD_DIM + ONES_ROWS, tm), BF16),
        jax.ShapeDtypeStruct((t, B_CHANNELS), F32),
        jax.ShapeDtypeStruct((t, C_QK), BF16),
        jax.ShapeDtypeStruct((t, C_QK), BF16),
        jax.ShapeDtypeStruct((BATCH, C_HEADS, nblk, C_V_DIM + ONES_ROWS, tm), BF16),
        jax.ShapeDtypeStruct((t, N_BRANCH * D_MODEL), BF16),
    )
    out_specs = (
        pl.BlockSpec((tm, A_Q), row),
        pl.BlockSpec((tm, A_KV), row),
        pl.BlockSpec((1, A_KV_HEADS, 1, HEAD_DIM + ONES_ROWS, tm), vt_map),
        pl.BlockSpec((tm, B_CHANNELS), row),
        pl.BlockSpec((tm, C_QK), row),
        pl.BlockSpec((tm, C_QK), row),
        pl.BlockSpec((1, C_HEADS, 1, C_V_DIM + ONES_ROWS, tm), vt_map),
        pl.BlockSpec((tm, N_BRANCH * D_MODEL), row),
    )
    in_specs = [
        pl.BlockSpec((tm, D_MODEL), row),
        pl.BlockSpec((1, 1, 6 * D_MODEL), lambda i: (mod_row(i), 0, 0)),
        _resident((1, D_MODEL)),
        _resident((D_MODEL, IN_WIDTH)),
        _resident((1, N_BRANCH * D_MODEL)),
        _resident((1, A_Q)),
        _resident((1, A_KV)),
        _resident((A_Q, A_Q)),
        pl.BlockSpec((tm, LANES), lambda i: (i % nblk, 0)),
        pl.BlockSpec((tm, LANES), lambda i: (i % nblk, 0)),
    ]
    return pl.pallas_call(
        functools.partial(_inproj_kernel, rope=rope, tm=tm),
        grid=(t // tm,),
        in_specs=in_specs,
        out_specs=out_specs,
        out_shape=out_shape,
        compiler_params=_cparams("parallel"),
        name="inproj_rope" if rope else "inproj_ctx",
    )(x, mod3, g1, w_in, b_gate, qn, kn, bd, cos_t, sin_t)


N_PROBLEMS = 8
STAGE_DELAYS = (3, 4, 5)


def _half_queries(q_ref, group, rows):
    qg = q_ref[:, group * LANES:(group + 1) * LANES]
    lane = lax.broadcasted_iota(jnp.int32, (rows, LANES), 1)
    zero = jnp.zeros_like(qg)
    return jnp.where(lane < HEAD_DIM, qg, zero), jnp.where(lane >= HEAD_DIM, qg, zero)


def _attend_all(q_ref, k_lat_ref, k_ctx_ref, vt_lat_ref, vt_ctx_ref, qx_ref, m_ref, acc_ref, k_lane0, vgroup, tq):
    for g in range(N_PROBLEMS // 2):
        qx_ref[2 * g], qx_ref[2 * g + 1] = _half_queries(q_ref, g, tq)
    m_ref[...] = jnp.full(m_ref.shape, -jnp.inf, F32)
    acc_ref[...] = jnp.zeros(acc_ref.shape, F32)
    dn = (((1,), (1,)), ((), ()))

    def step(load_k, load_vt):
        def scores(i):
            return lax.dot_general(load_k(k_lane0(i)), qx_ref[i], dn, preferred_element_type=F32)

        d_max, d_exp, d_pv = STAGE_DELAYS
        st, m_pair, p = {}, {}, {}
        for t in range(N_PROBLEMS + d_pv):
            if t < N_PROBLEMS:
                st[t] = scores(t)
            i = t - d_max
            if 0 <= i < N_PROBLEMS:
                m_old = m_ref[i]
                m_pair[i] = (m_old, jnp.maximum(m_old, jnp.max(st[i], axis=0, keepdims=True)))
            i = t - d_exp
            if 0 <= i < N_PROBLEMS:
                p[i] = jnp.exp2(st.pop(i) - m_pair[i][1]).astype(BF16)
            i = t - d_pv
            if 0 <= i < N_PROBLEMS:
                m_old, m_new = m_pair.pop(i)
                acc_ref[i] = jnp.exp2(m_old - m_new) * acc_ref[i] + _dot(load_vt(vgroup(i)), p.pop(i))
                m_ref[i] = m_new

    def body(c, carry):
        r0 = pl.multiple_of(c * KEY_CHUNK, KEY_CHUNK)
        step(lambda l0: k_lat_ref[0, pl.ds(r0, KEY_CHUNK), l0:l0 + LANES], lambda vg: vt_lat_ref[0, vg, c])
        return carry

    lax.fori_loop(0, k_lat_ref.shape[1] // KEY_CHUNK, body, 0)
    step(lambda l0: k_ctx_ref[0, :, l0:l0 + LANES], lambda vg: vt_ctx_ref[0, vg, 0])


def _gqa_kernel(q_ref, k_lat_ref, k_ctx_ref, vt_lat_ref, vt_ctx_ref, o_ref, qx_ref, m_ref, acc_ref, *, tq):
    _attend_all(q_ref, k_lat_ref, k_ctx_ref, vt_lat_ref, vt_ctx_ref, qx_ref, m_ref, acc_ref,
                lambda i: 0, lambda i: i % 2, tq)
    for pair in range(A_GROUP):
        outs = [acc_ref[2 * pair + h, 0:HEAD_DIM] / acc_ref[2 * pair + h, HEAD_DIM:HEAD_DIM + 1] for h in range(2)]
        o_ref[:, pair * LANES:(pair + 1) * LANES] = jnp.concatenate(outs, axis=0).T.astype(BF16)


def _diff_lambda(lq1_ref, lk1_ref, lq2_ref, lk2_ref, lam_init):
    return (jnp.exp(jnp.sum(lq1_ref[...] * lk1_ref[...], axis=-1, keepdims=True))
            - jnp.exp(jnp.sum(lq2_ref[...] * lk2_ref[...], axis=-1, keepdims=True)) + lam_init)


def _diff_combine(o1, o2, lam, subg, lam_init):
    o = o1 - lam * o2
    o = o * lax.rsqrt(jnp.mean(o * o, axis=0, keepdims=True) + EPS)
    return o.T * subg * (1.0 - lam_init)


def _diff_kernel(q_ref, k_lat_ref, k_ctx_ref, vt_lat_ref, vt_ctx_ref, lq1_ref, lk1_ref, lq2_ref, lk2_ref,
                 subg_ref, o_ref, qx_ref, m_ref, acc_ref, *, tq, lam_init):
    _attend_all(q_ref, k_lat_ref, k_ctx_ref, vt_lat_ref, vt_ctx_ref, qx_ref, m_ref, acc_ref,
                lambda i: (i // 2) * LANES, lambda i: i // 2, tq)
    lam = _diff_lambda(lq1_ref, lk1_ref, lq2_ref, lk2_ref, lam_init)
    for head in range(C_HEADS):
        outs = [acc_ref[2 * head + h, 0:C_V_DIM] / acc_ref[2 * head + h, C_V_DIM:C_V_DIM + 1] for h in range(2)]
        o_ref[:, head * LANES:(head + 1) * LANES] = _diff_combine(
            outs[0], outs[1], lam, subg_ref[...], lam_init).astype(BF16)


def _attention_call(kernel, name, q, k_lat, k_ctx, vt_lat, vt_ctx, extra, *, tokens_per_batch, tq):
    nq = tokens_per_batch // tq
    kw = k_lat.shape[-1]
    g, dvx = vt_lat.shape[1], vt_lat.shape[3]
    bmap3 = lambda b, i: (b, 0, 0)
    bmap5 = lambda b, i: (b, 0, 0, 0, 0)
    in_specs = [
        pl.BlockSpec((tq, q.shape[1]), lambda b, i: (b * nq + i, 0)),
        pl.BlockSpec((1, k_lat.shape[1], kw), bmap3),
        pl.BlockSpec((1, k_ctx.shape[1], kw), bmap3),
        pl.BlockSpec((1, g, vt_lat.shape[2], dvx, vt_lat.shape[4]), bmap5),
        pl.BlockSpec((1, g, 1, dvx, vt_ctx.shape[4]), bmap5),
    ] + [_resident(e.shape) for e in extra]
    return pl.pallas_call(
        kernel,
        grid=(BATCH, nq),
        in_specs=in_specs,
        out_specs=pl.BlockSpec((tq, q.shape[1]), lambda b, i: (b * nq + i, 0)),
        out_shape=jax.ShapeDtypeStruct(q.shape, BF16),
        scratch_shapes=[pltpu.VMEM((N_PROBLEMS, tq, LANES), BF16), pltpu.VMEM((N_PROBLEMS, 1, tq), F32),
                        pltpu.VMEM((N_PROBLEMS, dvx, tq), F32)],
        compiler_params=_cparams("parallel", "arbitrary"),
        name=name,
    )(q, k_lat, k_ctx, vt_lat, vt_ctx, *extra)


def _attend_ctx(q_ext, k, vt):
    dn = (((1,), (1,)), ((), ()))
    st = lax.dot_general(k, q_ext, dn, preferred_element_type=F32)
    p = jnp.exp2(st - jnp.max(st, axis=0, keepdims=True)).astype(BF16)
    return _dot(vt, p)


def _gqa_ctx_kernel(q_ref, k_ref, vt_ref, o_ref, *, tq):
    for pair in range(A_GROUP):
        outs = []
        for half, q_ext in enumerate(_half_queries(q_ref, pair, tq)):
            acc = _attend_ctx(q_ext, k_ref[0], vt_ref[0, half, 0])
            outs.append(acc[0:HEAD_DIM] / acc[HEAD_DIM:HEAD_DIM + 1])
        o_ref[:, pair * LANES:(pair + 1) * LANES] = jnp.concatenate(outs, axis=0).T.astype(BF16)


def _diff_ctx_kernel(q_ref, k_ref, vt_ref, lq1_ref, lk1_ref, lq2_ref, lk2_ref, subg_ref, o_ref, *, tq, lam_init):
    lam = _diff_lambda(lq1_ref, lk1_ref, lq2_ref, lk2_ref, lam_init)
    for head in range(C_HEADS):
        outs = []
        for q_ext in _half_queries(q_ref, head, tq):
            acc = _attend_ctx(q_ext, k_ref[0, :, head * LANES:(head + 1) * LANES], vt_ref[0, head, 0])
            outs.append(acc[0:C_V_DIM] / acc[C_V_DIM:C_V_DIM + 1])
        o_ref[:, head * LANES:(head + 1) * LANES] = _diff_combine(
            outs[0], outs[1], lam, subg_ref[...], lam_init).astype(BF16)


def _ctx_attention_call(kernel, name, q, k, vt, extra):
    tq = CTX_LEN
    g, dvx = vt.shape[1], vt.shape[3]
    in_specs = [
        pl.BlockSpec((tq, q.shape[1]), lambda b: (b, 0)),
        pl.BlockSpec((1, tq, k.shape[-1]), lambda b: (b, 0, 0)),
        pl.BlockSpec((1, g, 1, dvx, tq), lambda b: (b, 0, 0, 0, 0)),
    ] + [_resident(e.shape) for e in extra]
    return pl.pallas_call(
        kernel,
        grid=(BATCH,),
        in_specs=in_specs,
        out_specs=pl.BlockSpec((tq, q.shape[1]), lambda b: (b, 0)),
        out_shape=jax.ShapeDtypeStruct(q.shape, BF16),
        compiler_params=_cparams("parallel"),
        name=name,
    )(q, k, vt, *extra)


CONV_HALO = 16


def _conv_kernel(prev_ref, cur_ref, next_ref, w_ref, b_ref, g_ref, beta_ref, o_ref, win_ref, *, tc, nblk):
    j = pl.program_id(1)
    win_ref[0:CONV_HALO, :] = jnp.where(j > 0, prev_ref[0], 0.0)
    win_ref[CONV_HALO:CONV_HALO + tc, :] = cur_ref[0]
    win_ref[CONV_HALO + tc:CONV_HALO + tc + CONV_HALO, :] = jnp.where(j < nblk - 1, next_ref[0], 0.0)
    pad = B_CONV_W // 2
    acc = jnp.zeros((tc, B_CHANNELS), F32) + b_ref[...]
    for k in range(B_CONV_W):
        r0 = CONV_HALO - pad + k
        acc = acc + win_ref[r0:r0 + tc, :] * w_ref[k:k + 1, :]
    mu = jnp.mean(acc, axis=-1, keepdims=True)
    cen = acc - mu
    var = jnp.mean(cen * cen, axis=-1, keepdims=True)
    y = cen * lax.rsqrt(var + EPS) * g_ref[...] + beta_ref[...]
    o_ref[0] = _silu(y).astype(BF16)


def _conv_call(yb, dw_w, dw_b, ln_g, ln_b, *, tokens_per_batch):
    p = tokens_per_batch
    tc = 256
    nblk = p // tc
    r = tc // CONV_HALO
    nh = p // CONV_HALO
    y3 = yb.reshape(BATCH, p, B_CHANNELS)
    out = pl.pallas_call(
        functools.partial(_conv_kernel, tc=tc, nblk=nblk),
        grid=(BATCH, nblk),
        in_specs=[
            pl.BlockSpec((1, CONV_HALO, B_CHANNELS), lambda b, j: (b, jnp.maximum(j * r - 1, 0), 0)),
            pl.BlockSpec((1, tc, B_CHANNELS), lambda b, j: (b, j, 0)),
            pl.BlockSpec((1, CONV_HALO, B_CHANNELS), lambda b, j: (b, jnp.minimum((j + 1) * r, nh - 1), 0)),
            _resident((32, B_CHANNELS)),
            _resident((1, B_CHANNELS)),
            _resident((1, B_CHANNELS)),
            _resident((1, B_CHANNELS)),
        ],
        out_specs=pl.BlockSpec((1, tc, B_CHANNELS), lambda b, j: (b, j, 0)),
        out_shape=jax.ShapeDtypeStruct((BATCH, p, B_CHANNELS), BF16),
        scratch_shapes=[pltpu.VMEM((tc + 2 * CONV_HALO, B_CHANNELS), F32)],
        compiler_params=_cparams("parallel", "arbitrary"),
        name="conformer_conv",
    )(y3, y3, y3, dw_w, dw_b, ln_g, ln_b)
    return out.reshape(BATCH * p, B_CHANNELS)


def _merge_kernel(x_ref, mod_ref, oa_ref, ob_ref, oc_ref, gt_ref, wpa_ref, wpb_ref, wpc_ref, wout_ref, o_ref):
    d = D_MODEL
    m = (gt_ref[:, 0:d].astype(F32) * _dot(oa_ref[...], wpa_ref[...])
         + gt_ref[:, d:2 * d].astype(F32) * _dot(ob_ref[...], wpb_ref[...])
         + gt_ref[:, 2 * d:3 * d].astype(F32) * _dot(oc_ref[...], wpc_ref[...]))
    y = _dot(m.astype(BF16), wout_ref[...])
    gate1 = mod_ref[0][:, 2 * d:3 * d]
    o_ref[...] = x_ref[...] + gate1 * y


def _merge_call(x, mod3, mod_row, oa, ob, oc, gt, wpa, wpb, wpc, wout, *, tm):
    t = x.shape[0]
    row = lambda i: (i, 0)
    return pl.pallas_call(
        _merge_kernel,
        grid=(t // tm,),
        in_specs=[
            pl.BlockSpec((tm, D_MODEL), row),
            pl.BlockSpec((1, 1, 6 * D_MODEL), lambda i: (mod_row(i), 0, 0)),
            pl.BlockSpec((tm, A_Q), row),
            pl.BlockSpec((tm, B_CHANNELS), row),
            pl.BlockSpec((tm, C_V), row),
            pl.BlockSpec((tm, N_BRANCH * D_MODEL), row),
            _resident(wpa.shape), _resident(wpb.shape), _resident(wpc.shape), _resident(wout.shape),
        ],
        out_specs=pl.BlockSpec((tm, D_MODEL), row),
        out_shape=jax.ShapeDtypeStruct(x.shape, F32),
        compiler_params=_cparams("parallel"),
        name="branch_merge",
    )(x, mod3, oa, ob, oc, gt, wpa, wpb, wpc, wout)


FFN_SPLIT = 2


def _ffn_kernel(x_ref, mod_ref, g2_ref, w1_ref, w3_ref, w2_ref, o_ref):
    d = D_MODEL
    mod = mod_ref[0]
    x = x_ref[...]
    h = _norm_mod(x, g2_ref[...], mod[:, 3 * d:4 * d], mod[:, 4 * d:5 * d]).astype(BF16)
    fc = D_FF // FFN_SPLIT
    y = jnp.zeros(x.shape, F32)
    for s in range(FFN_SPLIT):
        a = _dot(h, w1_ref[:, s * fc:(s + 1) * fc])
        b = _dot(h, w3_ref[:, s * fc:(s + 1) * fc])
        y = y + _dot((_silu(a) * b).astype(BF16), w2_ref[s * fc:(s + 1) * fc, :])
    o_ref[...] = x + mod[:, 5 * d:6 * d] * y


def _ffn_call(x, mod3, mod_row, g2, w1, w3, w2, *, tm):
    t = x.shape[0]
    row = lambda i: (i, 0)
    return pl.pallas_call(
        _ffn_kernel,
        grid=(t // tm,),
        in_specs=[
            pl.BlockSpec((tm, D_MODEL), row),
            pl.BlockSpec((1, 1, 6 * D_MODEL), lambda i: (mod_row(i), 0, 0)),
            _resident((1, D_MODEL)),
            _resident(w1.shape), _resident(w3.shape), _resident(w2.shape),
        ],
        out_specs=pl.BlockSpec((tm, D_MODEL), row),
        out_shape=jax.ShapeDtypeStruct(x.shape, F32),
        compiler_params=_cparams("parallel"),
        name="dense_swiglu",
    )(x, mod3, g2, w1, w3, w2)


MOE_F_TILE = 1792
MOE_TM = 512
TOP_K = 2


def _router_kernel(x_ref, mod_ref, g2_ref, wr_ref, br_ref, h_ref, sel_ref, wts_ref):
    d = D_MODEL
    mod = mod_ref[0]
    h = _norm_mod(x_ref[...], g2_ref[...], mod[:, 3 * d:4 * d], mod[:, 4 * d:5 * d])
    h_ref[...] = h
    logits = jnp.dot(h, wr_ref[...], preferred_element_type=F32, precision=lax.Precision.HIGHEST) + br_ref[...]
    lane = lax.broadcasted_iota(jnp.int32, logits.shape, 1)
    v1 = jnp.max(logits, axis=-1, keepdims=True)
    i1 = jnp.min(jnp.where(logits == v1, lane, LANES), axis=-1, keepdims=True)
    rest = jnp.where(lane == i1, -jnp.inf, logits)
    v2 = jnp.max(rest, axis=-1, keepdims=True)
    i2 = jnp.min(jnp.where(rest == v2, lane, LANES), axis=-1, keepdims=True)
    e2 = jnp.exp(v2 - v1)
    w1 = 1.0 / (1.0 + e2)
    sel_ref[...] = jnp.where(lane == 0, i1, jnp.where(lane == 1, i2, 0))
    wts_ref[...] = jnp.where(lane == 0, w1, jnp.where(lane == 1, e2 * w1, 0.0))


def _router_call(x, mod3, mod_row, g2, wr, br, *, tm):
    t = x.shape[0]
    row = lambda i: (i, 0)
    return pl.pallas_call(
        _router_kernel,
        grid=(t // tm,),
        in_specs=[
            pl.BlockSpec((tm, D_MODEL), row),
            pl.BlockSpec((1, 1, 6 * D_MODEL), lambda i: (mod_row(i), 0, 0)),
            _resident((1, D_MODEL)), _resident((D_MODEL, LANES)), _resident((1, LANES)),
        ],
        out_specs=(pl.BlockSpec((tm, D_MODEL), row), pl.BlockSpec((tm, LANES), row), pl.BlockSpec((tm, LANES), row)),
        out_shape=(jax.ShapeDtypeStruct((t, D_MODEL), F32), jax.ShapeDtypeStruct((t, LANES), jnp.int32),
                   jax.ShapeDtypeStruct((t, LANES), F32)),
        compiler_params=_cparams("parallel"),
        name="moe_router",
    )(x, mod3, g2, wr, br)


def _dispatch_plan(sel, n_rows):
    e_flat = sel.reshape(-1)
    n_assign = e_flat.shape[0]
    onehot = (e_flat[:, None] == jnp.arange(N_EXPERTS, dtype=jnp.int32)[None, :]).astype(jnp.int32)
    csum = jnp.cumsum(onehot, axis=0)
    rank = jnp.take_along_axis(csum, e_flat[:, None], axis=1)[:, 0] - 1
    padded = (csum[-1] + MOE_TM - 1) // MOE_TM * MOE_TM
    end = jnp.cumsum(padded)
    pos = (end - padded)[e_flat] + rank
    row_token = jnp.zeros((n_rows,), jnp.int32).at[pos].set(jnp.arange(n_assign, dtype=jnp.int32) // TOP_K)
    tile_start = jnp.arange(n_rows // MOE_TM, dtype=jnp.int32) * MOE_TM
    tile_expert = jnp.minimum(jnp.sum(tile_start[:, None] >= end[None, :], axis=1), N_EXPERTS - 1).astype(jnp.int32)
    n_tiles = (end[-1:] // MOE_TM).astype(jnp.int32)
    return row_token, pos.astype(jnp.int32), tile_expert, n_tiles


def _gather_rows(idx_ref, idx0, src_hbm, dst_ref, sem, row0, n, inline=False):
    def body(r, carry):
        src = src_hbm.at[pl.ds(idx_ref[idx0 + r], 1)]
        pltpu.make_async_copy(src, dst_ref.at[pl.ds(row0 + r, 1)], sem).start()
        return carry
    if inline:
        for r in range(n):
            body(r, 0)
    else:
        lax.fori_loop(0, n, body, 0, unroll=8)


def _wait_rows(src_hbm, dst_ref, sem):
    pltpu.make_async_copy(src_hbm.at[pl.ds(0, dst_ref.shape[0])], dst_ref, sem).wait()


def _expert_kernel(row_token_ref, tile_expert_ref, n_tiles_ref, h_hbm, w1_ref, w3_ref, w2_ref, y_ref,
                   xbuf_ref, acc_ref, sem_ref):
    j = pl.program_id(0)
    f = pl.program_id(1)
    nf = pl.num_programs(1)
    slot = j % 2
    valid = j < n_tiles_ref[0]

    @pl.when(jnp.logical_and(j == 0, f == 0))
    def _():
        _gather_rows(row_token_ref, 0, h_hbm, xbuf_ref.at[0], sem_ref.at[0], 0, MOE_TM)

    @pl.when(jnp.logical_and(valid, f == 0))
    def _():
        _wait_rows(h_hbm, xbuf_ref.at[slot], sem_ref.at[slot])

    def compute(gather_next):
        if gather_next:
            share = MOE_TM // (D_FF_EXPERT // MOE_F_TILE)
            _gather_rows(row_token_ref, (j + 1) * MOE_TM + f * share, h_hbm, xbuf_ref.at[1 - slot],
                         sem_ref.at[1 - slot], f * share, share, inline=True)
        x = xbuf_ref[slot].astype(BF16)
        z = (_silu(_dot(x, w1_ref[0])) * _dot(x, w3_ref[0])).astype(BF16)
        part = _dot(z, w2_ref[0])

        @pl.when(f == 0)
        def _():
            acc_ref[...] = part

        @pl.when(f > 0)
        def _():
            acc_ref[...] += part

        @pl.when(f == nf - 1)
        def _():
            y_ref[...] = acc_ref[...]

    has_next = j + 1 < n_tiles_ref[0]
    pl.when(has_next)(functools.partial(compute, True))
    pl.when(jnp.logical_and(valid, jnp.logical_not(has_next)))(functools.partial(compute, False))

    @pl.when(jnp.logical_and(jnp.logical_not(valid), f == nf - 1))
    def _():
        y_ref[...] = jnp.zeros(y_ref.shape, F32)


def _expert_call(h, row_token, tile_expert, n_tiles, w1, w3, w2):
    n_rows = row_token.shape[0]
    nt = n_rows // MOE_TM
    nf = D_FF_EXPERT // MOE_F_TILE

    def f_eff(j, f, n):
        return jnp.where(j < n[0], f, nf - 1)

    grid_spec = pltpu.PrefetchScalarGridSpec(
        num_scalar_prefetch=3,
        grid=(nt, nf),
        in_specs=[
            pl.BlockSpec(memory_space=pl.ANY),
            pl.BlockSpec((1, D_MODEL, MOE_F_TILE), lambda j, f, rt, te, n: (te[j], 0, f_eff(j, f, n))),
            pl.BlockSpec((1, D_MODEL, MOE_F_TILE), lambda j, f, rt, te, n: (te[j], 0, f_eff(j, f, n))),
            pl.BlockSpec((1, MOE_F_TILE, D_MODEL), lambda j, f, rt, te, n: (te[j], f_eff(j, f, n), 0)),
        ],
        out_specs=pl.BlockSpec((MOE_TM, D_MODEL), lambda j, f, rt, te, n: (j, 0)),
        scratch_shapes=[pltpu.VMEM((2, MOE_TM, D_MODEL), F32), pltpu.VMEM((MOE_TM, D_MODEL), F32),
                        pltpu.SemaphoreType.DMA((2,))],
    )
    return pl.pallas_call(
        _expert_kernel,
        grid_spec=grid_spec,
        out_shape=jax.ShapeDtypeStruct((n_rows, D_MODEL), F32),
        compiler_params=_cparams("arbitrary", "arbitrary"),
        name="moe_experts",
    )(row_token, tile_expert, n_tiles, h, w1, w3, w2)


def _combine_kernel(pos_ref, y_hbm, x_ref, mod_ref, wts_ref, fg_ref, o_ref, ybuf_ref, sem_ref, *, tc, final_norm):
    i = pl.program_id(0)
    slot = i % 2
    rows = TOP_K * tc

    @pl.when(i == 0)
    def _():
        _gather_rows(pos_ref, 0, y_hbm, ybuf_ref.at[0], sem_ref.at[0], 0, rows)

    @pl.when(i + 1 < pl.num_programs(0))
    def _():
        _gather_rows(pos_ref, (i + 1) * rows, y_hbm, ybuf_ref.at[1 - slot], sem_ref.at[1 - slot], 0, rows)

    _wait_rows(y_hbm, ybuf_ref.at[slot], sem_ref.at[slot])
    wts = wts_ref[...]
    moe = wts[:, 0:1] * ybuf_ref[slot, 0:tc, :] + wts[:, 1:2] * ybuf_ref[slot, tc:rows, :]
    y = x_ref[...] + mod_ref[0][:, 5 * D_MODEL:6 * D_MODEL] * moe
    if final_norm:
        y = y * lax.rsqrt(jnp.mean(y * y, axis=-1, keepdims=True) + EPS) * fg_ref[...]
    o_ref[...] = y


def _combine_call(x, mod3, mod_row, y, pos, wts, fg, *, tc, final_norm):
    t = x.shape[0]
    row = lambda i, p: (i, 0)
    pos_tiles = pos.reshape(t // tc, tc, TOP_K).transpose(0, 2, 1).reshape(-1)
    grid_spec = pltpu.PrefetchScalarGridSpec(
        num_scalar_prefetch=1,
        grid=(t // tc,),
        in_specs=[
            pl.BlockSpec(memory_space=pl.ANY),
            pl.BlockSpec((tc, D_MODEL), row),
            pl.BlockSpec((1, 1, 6 * D_MODEL), lambda i, p: (mod_row(i), 0, 0)),
            pl.BlockSpec((tc, LANES), row),
            pl.BlockSpec((1, D_MODEL), lambda i, p: (0, 0)),
        ],
        out_specs=pl.BlockSpec((tc, D_MODEL), row),
        scratch_shapes=[pltpu.VMEM((2, TOP_K * tc, D_MODEL), F32), pltpu.SemaphoreType.DMA((2,))],
    )
    return pl.pallas_call(
        functools.partial(_combine_kernel, tc=tc, final_norm=final_norm),
        grid_spec=grid_spec,
        out_shape=jax.ShapeDtypeStruct(x.shape, F32),
        compiler_params=_cparams("arbitrary"),
        name="moe_combine",
    )(pos_tiles, y, x, mod3, wts, fg)


def _moe_call(x, mod3, mod_row, g2, wr, br, w1, w3, w2, fg, *, tm, final_norm):
    t = x.shape[0]
    h, sel, wts = _router_call(x, mod3, mod_row, g2, wr, br, tm=tm)
    n_rows = TOP_K * t + N_EXPERTS * MOE_TM
    row_token, pos, tile_expert, n_tiles = _dispatch_plan(sel[:, :TOP_K], n_rows)
    y = _expert_call(h, row_token, tile_expert, n_tiles, w1, w3, w2)
    tc = 256
    return _combine_call(x, mod3, lambda i: mod_row(i * tc // tm), y, pos, wts, fg, tc=tc, final_norm=final_norm)


def _final_norm_kernel(x_ref, g_ref, o_ref):
    x = x_ref[...]
    o_ref[...] = x * lax.rsqrt(jnp.mean(x * x, axis=-1, keepdims=True) + EPS) * g_ref[...]


def _final_norm_call(x, g, *, tm):
    row = lambda i: (i, 0)
    return pl.pallas_call(
        _final_norm_kernel,
        grid=(x.shape[0] // tm,),
        in_specs=[pl.BlockSpec((tm, D_MODEL), row), _resident((1, D_MODEL))],
        out_specs=pl.BlockSpec((tm, D_MODEL), row),
        out_shape=jax.ShapeDtypeStruct(x.shape, F32),
        compiler_params=_cparams("parallel"),
        name="final_rmsnorm",
    )(x, g)


def _rope_tables():
    rows = SEQ // GRID_W
    row = jnp.repeat(jnp.arange(rows), GRID_W)
    col = jnp.tile(jnp.arange(GRID_W), rows)
    n_freq = HEAD_DIM // 4
    inv = ROPE_THETA ** (-jnp.arange(n_freq, dtype=F32) / n_freq)
    ang = jnp.stack([row, col], axis=-1).astype(F32)[:, :, None] * inv
    cos, sin = jnp.cos(ang), jnp.sin(ang)
    cos64 = jnp.concatenate([cos[:, 0], cos[:, 0], cos[:, 1], cos[:, 1]], axis=-1)
    sin64 = jnp.concatenate([-sin[:, 0], sin[:, 0], -sin[:, 1], sin[:, 1]], axis=-1)
    return jnp.tile(cos64, (1, 2)), jnp.tile(sin64, (1, 2))


def _pair_heads(w, axis):
    shp = w.shape
    w = w.reshape(shp[:axis] + (A_KV_HEADS, A_GROUP, HEAD_DIM) + shp[axis + 1:])
    w = jnp.swapaxes(w, axis, axis + 1)
    return w.reshape(shp)


def kernel(x, c, ctx, c_ctx, w_mod, b_mod, norm1_g, norm2_g, w_in, b_gate, a_qn_g, a_kn_g, b_dw_w, b_dw_b,
           b_ln_g, b_ln_b, c_lq1, c_lk1, c_lq2, c_lk2, c_subln_g, w_pa, w_pb, w_pc, w_out, ffn_w1, ffn_w3,
           ffn_w2, moe_router, moe_router_b, moe_w1, moe_w3, moe_w2, final_g):
    d = D_MODEL
    t_lat = BATCH * SEQ
    xl = x.reshape(t_lat, d)
    xc = ctx.reshape(BATCH * CTX_LEN, d)

    cc = jnp.concatenate([c, c_ctx[None, :], jnp.zeros((3, d), F32)], axis=0)
    mod_all = _mod_call(cc, w_mod, b_mod)

    cos_t, sin_t = _rope_tables()
    head_id = np.arange(A_Q) // HEAD_DIM
    bd = jnp.asarray(head_id[:, None] == head_id[None, :], BF16)

    lat_tiles_per_batch = SEQ // KEY_CHUNK
    lat_row = lambda i: i // lat_tiles_per_batch
    ctx_row = lambda i: 4
    tm = KEY_CHUNK

    for i in range(DEPTH):
        last = i == DEPTH - 1
        lam_init = 0.8 - 0.6 * math.exp(-0.3 * i)
        mod3 = mod_all[i].reshape(8, 1, 6 * d)
        g1 = norm1_g[i].reshape(1, d)
        g2 = norm2_g[i].reshape(1, d)
        w_in_i = jnp.concatenate([_pair_heads(w_in[i][:, :A_Q], 1), w_in[i][:, A_Q:]], axis=1).astype(BF16)
        bg = b_gate[i].reshape(1, -1)
        qn = jnp.tile(a_qn_g[i], A_HEADS).reshape(1, A_Q)
        kn = jnp.tile(a_kn_g[i], A_KV_HEADS).reshape(1, A_KV)
        lam_vecs = [v[i].reshape(1, HEAD_DIM) for v in (c_lq1, c_lk1, c_lq2, c_lk2)]
        subg = c_subln_g[i].reshape(1, C_V_DIM)
        wpa = _pair_heads(w_pa[i], 0).astype(BF16)
        wpb, wpc, wout = w_pb[i].astype(BF16), w_pc[i].astype(BF16), w_out[i].astype(BF16)
        dw_w = jnp.concatenate([b_dw_w[i], jnp.zeros((1, B_CHANNELS), F32)], axis=0)
        conv_p = (dw_w, b_dw_b[i].reshape(1, -1), b_ln_g[i].reshape(1, -1), b_ln_b[i].reshape(1, -1))

        proj = functools.partial(_inproj_call, g1=g1, w_in=w_in_i, b_gate=bg, qn=qn, kn=kn, bd=bd,
                                 cos_t=cos_t, sin_t=sin_t)
        qa, ka, vta, yb, qc, kc, vtc, gt = proj(xl, mod3, lat_row, tokens_per_batch=SEQ, rope=True)
        qa_c, ka_c, vta_c, yb_c, qc_c, kc_c, vtc_c, gt_c = proj(xc, mod3, ctx_row, tokens_per_batch=CTX_LEN,
                                                                rope=False)
        ka3, kc3 = ka.reshape(BATCH, SEQ, A_KV), kc.reshape(BATCH, SEQ, C_QK)
        ka3_c, kc3_c = ka_c.reshape(BATCH, CTX_LEN, A_KV), kc_c.reshape(BATCH, CTX_LEN, C_QK)

        oa = _attention_call(functools.partial(_gqa_kernel, tq=256), "gqa_attention", qa, ka3, ka3_c, vta, vta_c,
                             [], tokens_per_batch=SEQ, tq=256)
        oc = _attention_call(functools.partial(_diff_kernel, tq=256, lam_init=lam_init), "diff_attention",
                             qc, kc3, kc3_c, vtc, vtc_c, lam_vecs + [subg], tokens_per_batch=SEQ, tq=256)
        ob = _conv_call(yb, *conv_p, tokens_per_batch=SEQ)
        xl = _merge_call(xl, mod3, lat_row, oa, ob, oc, gt, wpa, wpb, wpc, wout, tm=tm)

        if not last:
            oa_c = _ctx_attention_call(functools.partial(_gqa_ctx_kernel, tq=CTX_LEN), "gqa_attention_ctx",
                                       qa_c, ka3_c, vta_c, [])
            oc_c = _ctx_attention_call(functools.partial(_diff_ctx_kernel, tq=CTX_LEN, lam_init=lam_init),
                                       "diff_attention_ctx", qc_c, kc3_c, vtc_c, lam_vecs + [subg])
            ob_c = _conv_call(yb_c, *conv_p, tokens_per_batch=CTX_LEN)
            xc = _merge_call(xc, mod3, ctx_row, oa_c, ob_c, oc_c, gt_c, wpa, wpb, wpc, wout, tm=tm)

        j = i // 2
        if i % 2 == 0:
            ffn_w = (ffn_w1[j].astype(BF16), ffn_w3[j].astype(BF16), ffn_w2[j].astype(BF16))
            xl = _ffn_call(xl, mod3, lat_row, g2, *ffn_w, tm=tm)
            if last:
                xl = _final_norm_call(xl, final_g.reshape(1, d), tm=tm)
            else:
                xc = _ffn_call(xc, mod3, ctx_row, g2, *ffn_w, tm=tm)
        else:
            wr = jnp.zeros((d, LANES), F32).at[:, :N_EXPERTS].set(moe_router[j])
            br = jnp.full((1, LANES), -jnp.inf, F32).at[0, :N_EXPERTS].set(moe_router_b[j])
            moe_w = (moe_w1[j].astype(BF16), moe_w3[j].astype(BF16), moe_w2[j].astype(BF16))
            fg = final_g.reshape(1, d)
            xl = _moe_call(xl, mod3, lat_row, g2, wr, br, *moe_w, fg, tm=tm, final_norm=last)
            if not last:
                xc = _moe_call(xc, mod3, ctx_row, g2, wr, br, *moe_w, fg, tm=tm, final_norm=False)
    return xl.reshape(BATCH, SEQ, d)
```

```python
import functools
import math

import numpy as np
import jax
import jax.numpy as jnp
from jax import lax
from jax.experimental import pallas as pl
from jax.experimental.pallas import tpu as pltpu

F32 = jnp.float32
BF16 = jnp.bfloat16

D_MODEL = 1024
BATCH = 4
SEQ = 4096
DEPTH = 2
GRID_W = 64
CTX_LEN = 256
EPS = 1e-6
ROPE_THETA = 10000.0

HEAD_DIM = 64
A_HEADS = 8
A_KV_HEADS = 2
A_GROUP = 4
A_Q = 512
A_KV = 128
B_CHANNELS = 512
B_CONV_W = 31
C_HEADS = 4
C_V_DIM = 128
C_QK = 512
C_V = 512
N_BRANCH = 3
IN_SPLITS = [A_Q, A_KV, A_KV, 2 * B_CHANNELS, C_QK, C_QK, C_V, N_BRANCH * D_MODEL]
IN_WIDTH = sum(IN_SPLITS)
IN_OFF = [0] + [int(v) for v in np.cumsum(IN_SPLITS)]
D_FF = 2816
N_EXPERTS = 8
D_FF_EXPERT = 3584

Q_SCALE = HEAD_DIM ** -0.5 * 1.4426950408889634

LANES = 128
ONES_ROWS = 16
KEY_CHUNK = 512
VMEM_LIMIT = 56 * 1024 * 1024


def _cparams(*sem):
    return pltpu.CompilerParams(dimension_semantics=sem, vmem_limit_bytes=VMEM_LIMIT)


def _resident(shape):
    nd = len(shape)
    return pl.BlockSpec(shape, lambda *_: (0,) * nd, pipeline_mode=pl.Buffered(1))


def _sigmoid(v):
    return 1.0 / (1.0 + jnp.exp(-v))


def _silu(v):
    return v * _sigmoid(v)


def _norm_mod(x, g, shift, scale):
    ms = jnp.mean(x * x, axis=-1, keepdims=True)
    return (x * lax.rsqrt(ms + EPS) * g) * (1.0 + scale) + shift


def _dot(a, b):
    return jnp.dot(a, b, preferred_element_type=F32)


def _mod_kernel(c_ref, w_ref, b_ref, o_ref):
    s = _silu(c_ref[...])
    o_ref[0] = jnp.dot(s, w_ref[0], preferred_element_type=F32, precision=lax.Precision.HIGHEST) + b_ref[0]


def _mod_call(cc, w_mod, b_mod):
    n = 6 * D_MODEL
    tn = D_MODEL
    return pl.pallas_call(
        _mod_kernel,
        grid=(DEPTH, n // tn),
        in_specs=[
            pl.BlockSpec((8, D_MODEL), lambda i, j: (0, 0)),
            pl.BlockSpec((1, D_MODEL, tn), lambda i, j: (i, 0, j)),
            pl.BlockSpec((1, 1, tn), lambda i, j: (i, 0, j)),
        ],
        out_specs=pl.BlockSpec((1, 8, tn), lambda i, j: (i, 0, j)),
        out_shape=jax.ShapeDtypeStruct((DEPTH, 8, n), F32),
        compiler_params=_cparams("arbitrary", "arbitrary"),
        name="adaln_mod",
    )(cc, w_mod, b_mod.reshape(DEPTH, 1, n))


def _segment_mean_sq(u, bd):
    sq = u * u
    hi = sq.astype(BF16)
    lo = (sq - hi.astype(F32)).astype(BF16)
    return (_dot(hi, bd) + _dot(lo, bd)) * (1.0 / HEAD_DIM)


def _rope_group(xg, cos, sin, lo_mask):
    partner = jnp.where(lo_mask, pltpu.roll(xg, LANES - 16, 1), pltpu.roll(xg, 16, 1))
    return xg * cos + partner * sin


def _inproj_kernel(x_ref, mod_ref, g1_ref, w_ref, bgate_ref, qn_ref, kn_ref, bd_ref, cos_ref, sin_ref,
                   qa_ref, ka_ref, vta_ref, yb_ref, qc_ref, kc_ref, vtc_ref, gt_ref, *, rope, tm):
    mod = mod_ref[0]
    h = _norm_mod(x_ref[...], g1_ref[...], mod[:, 0:D_MODEL], mod[:, D_MODEL:2 * D_MODEL]).astype(BF16)

    def proj(c0, c1):
        return _dot(h, w_ref[:, c0:c1])

    if rope:
        cos = cos_ref[...]
        sin = sin_ref[...]
        lane = lax.broadcasted_iota(jnp.int32, (tm, LANES), 1)
        lo_mask = (lane % 32) < 16

    def rot(u, j):
        ug = u[:, j * LANES:(j + 1) * LANES]
        return _rope_group(ug, cos, sin, lo_mask) if rope else ug

    ones_pat = jnp.where(lax.broadcasted_iota(jnp.int32, (ONES_ROWS, tm), 0) == 0, 1.0, 0.0).astype(BF16)

    u = proj(IN_OFF[0], IN_OFF[1])
    u = u * lax.rsqrt(_segment_mean_sq(u, bd_ref[...]) + EPS) * (qn_ref[...] * Q_SCALE)
    for j in range(A_Q // LANES):
        qa_ref[:, j * LANES:(j + 1) * LANES] = rot(u, j).astype(BF16)
    u = proj(IN_OFF[1], IN_OFF[2])
    u = u * lax.rsqrt(_segment_mean_sq(u, bd_ref[0:A_KV, 0:A_KV]) + EPS) * kn_ref[...]
    ka_ref[...] = rot(u, 0).astype(BF16)
    ut = proj(IN_OFF[2], IN_OFF[3]).T
    for j in range(A_KV_HEADS):
        vta_ref[0, j, 0, 0:HEAD_DIM, :] = ut[j * HEAD_DIM:(j + 1) * HEAD_DIM].astype(BF16)
        vta_ref[0, j, 0, HEAD_DIM:HEAD_DIM + ONES_ROWS, :] = ones_pat
    a = proj(IN_OFF[3], IN_OFF[3] + B_CHANNELS)
    g = proj(IN_OFF[3] + B_CHANNELS, IN_OFF[4])
    yb_ref[...] = a * _sigmoid(g)
    u = proj(IN_OFF[4], IN_OFF[5]) * Q_SCALE
    for j in range(C_QK // LANES):
        qc_ref[:, j * LANES:(j + 1) * LANES] = rot(u, j).astype(BF16)
    u = proj(IN_OFF[5], IN_OFF[6])
    for j in range(C_QK // LANES):
        kc_ref[:, j * LANES:(j + 1) * LANES] = rot(u, j).astype(BF16)
    ut = proj(IN_OFF[6], IN_OFF[7]).T
    for j in range(C_HEADS):
        vtc_ref[0, j, 0, 0:C_V_DIM, :] = ut[j * C_V_DIM:(j + 1) * C_V_DIM].astype(BF16)
        vtc_ref[0, j, 0, C_V_DIM:C_V_DIM + ONES_ROWS, :] = ones_pat
    for j in range(N_BRANCH):
        c0 = IN_OFF[7] + j * D_MODEL
        gt_ref[:, j * D_MODEL:(j + 1) * D_MODEL] = _sigmoid(
            proj(c0, c0 + D_MODEL) + bgate_ref[:, j * D_MODEL:(j + 1) * D_MODEL]).astype(BF16)


def _inproj_call(x, mod3, mod_row, g1, w_in, b_gate, qn, kn, bd, cos_t, sin_t, *, tokens_per_batch, rope):
    p = tokens_per_batch
    tm = min(p, KEY_CHUNK)
    nblk = p // tm
    t = BATCH * p
    row = lambda i: (i, 0)
    vt_map = lambda i: (i // nblk, 0, i % nblk, 0, 0)
    out_shape = (
        jax.ShapeDtypeStruct((t, A_Q), BF16),
        jax.ShapeDtypeStruct((t, A_KV), BF16),
        jax.ShapeDtypeStruct((BATCH, A_KV_HEADS, nblk, HEAD_DIM + ONES_ROWS, tm), BF16),
        jax.ShapeDtypeStruct((t, B_CHANNELS), F32),
        jax.ShapeDtypeStruct((t, C_QK), BF16),
        jax.ShapeDtypeStruct((t, C_QK), BF16),
        jax.ShapeDtypeStruct((BATCH, C_HEADS, nblk, C_V_DIM + ONES_ROWS, tm), BF16),
        jax.ShapeDtypeStruct((t, N_BRANCH * D_MODEL), BF16),
    )
    out_specs = (
        pl.BlockSpec((tm, A_Q), row),
        pl.BlockSpec((tm, A_KV), row),
        pl.BlockSpec((1, A_KV_HEADS, 1, HEAD_DIM + ONES_ROWS, tm), vt_map),
        pl.BlockSpec((tm, B_CHANNELS), row),
        pl.BlockSpec((tm, C_QK), row),
        pl.BlockSpec((tm, C_QK), row),
        pl.BlockSpec((1, C_HEADS, 1, C_V_DIM + ONES_ROWS, tm), vt_map),
        pl.BlockSpec((tm, N_BRANCH * D_MODEL), row),
    )
    in_specs = [
        pl.BlockSpec((tm, D_MODEL), row),
        pl.BlockSpec((1, 1, 6 * D_MODEL), lambda i: (mod_row(i), 0, 0)),
        _resident((1, D_MODEL)),
        _resident((D_MODEL, IN_WIDTH)),
        _resident((1, N_BRANCH * D_MODEL)),
        _resident((1, A_Q)),
        _resident((1, A_KV)),
        _resident((A_Q, A_Q)),
        pl.BlockSpec((tm, LANES), lambda i: (i % nblk, 0)),
        pl.BlockSpec((tm, LANES), lambda i: (i % nblk, 0)),
    ]
    return pl.pallas_call(
        functools.partial(_inproj_kernel, rope=rope, tm=tm),
        grid=(t // tm,),
        in_specs=in_specs,
        out_specs=out_specs,
        out_shape=out_shape,
        compiler_params=_cparams("parallel"),
        name="inproj_rope" if rope else "inproj_ctx",
    )(x, mod3, g1, w_in, b_gate, qn, kn, bd, cos_t, sin_t)


N_PROBLEMS = 8
STAGE_DELAYS = (3, 4, 5)
CHUNK_UNROLL = 8


def _half_queries(q_ref, group, rows):
    qg = q_ref[:, group * LANES:(group + 1) * LANES]
    lane = lax.broadcasted_iota(jnp.int32, (rows, LANES), 1)
    zero = jnp.zeros_like(qg)
    return jnp.where(lane < HEAD_DIM, qg, zero), jnp.where(lane >= HEAD_DIM, qg, zero)


def _attend_all(q_ref, k_lat_ref, k_ctx_ref, vt_lat_ref, vt_ctx_ref, qx_ref, m_ref, acc_ref, k_lane0, vgroup, tq):
    for g in range(N_PROBLEMS // 2):
        qx_ref[2 * g], qx_ref[2 * g + 1] = _half_queries(q_ref, g, tq)
    m_ref[...] = jnp.full(m_ref.shape, -jnp.inf, F32)
    acc_ref[...] = jnp.zeros(acc_ref.shape, F32)
    dn = (((1,), (1,)), ((), ()))

    def step(load_k, load_vt):
        def scores(i):
            return lax.dot_general(load_k(k_lane0(i)), qx_ref[i], dn, preferred_element_type=F32)

        d_max, d_exp, d_pv = STAGE_DELAYS
        st, m_pair, p = {}, {}, {}
        for t in range(N_PROBLEMS + d_pv):
            if t < N_PROBLEMS:
                st[t] = scores(t)
            i = t - d_max
            if 0 <= i < N_PROBLEMS:
                m_old = m_ref[i]
                m_pair[i] = (m_old, jnp.maximum(m_old, jnp.max(st[i], axis=0, keepdims=True)))
            i = t - d_exp
            if 0 <= i < N_PROBLEMS:
                p[i] = jnp.exp2((st.pop(i) - m_pair[i][1]).astype(BF16))
            i = t - d_pv
            if 0 <= i < N_PROBLEMS:
                m_old, m_new = m_pair.pop(i)
                acc_ref[i] = jnp.exp2(m_old - m_new) * acc_ref[i] + _dot(load_vt(vgroup(i)), p.pop(i))
                m_ref[i] = m_new

    def body(c, carry):
        r0 = pl.multiple_of(c * KEY_CHUNK, KEY_CHUNK)
        step(lambda l0: k_lat_ref[0, pl.ds(r0, KEY_CHUNK), l0:l0 + LANES], lambda vg: vt_lat_ref[0, vg, c])
        return carry

    lax.fori_loop(0, k_lat_ref.shape[1] // KEY_CHUNK, body, 0, unroll=CHUNK_UNROLL)
    step(lambda l0: k_ctx_ref[0, :, l0:l0 + LANES], lambda vg: vt_ctx_ref[0, vg, 0])


def _gqa_kernel(q_ref, k_lat_ref, k_ctx_ref, vt_lat_ref, vt_ctx_ref, o_ref, qx_ref, m_ref, acc_ref, *, tq):
    _attend_all(q_ref, k_lat_ref, k_ctx_ref, vt_lat_ref, vt_ctx_ref, qx_ref, m_ref, acc_ref,
                lambda i: 0, lambda i: i % 2, tq)
    for pair in range(A_GROUP):
        outs = [acc_ref[2 * pair + h, 0:HEAD_DIM] / acc_ref[2 * pair + h, HEAD_DIM:HEAD_DIM + 1] for h in range(2)]
        o_ref[:, pair * LANES:(pair + 1) * LANES] = jnp.concatenate(outs, axis=0).T.astype(BF16)


def _diff_lambda(lq1_ref, lk1_ref, lq2_ref, lk2_ref, lam_init):
    return (jnp.exp(jnp.sum(lq1_ref[...] * lk1_ref[...], axis=-1, keepdims=True))
            - jnp.exp(jnp.sum(lq2_ref[...] * lk2_ref[...], axis=-1, keepdims=True)) + lam_init)


def _diff_combine(o1, o2, lam, subg, lam_init):
    o = o1 - lam * o2
    o = o * lax.rsqrt(jnp.mean(o * o, axis=0, keepdims=True) + EPS)
    return o.T * subg * (1.0 - lam_init)


def _diff_kernel(q_ref, k_lat_ref, k_ctx_ref, vt_lat_ref, vt_ctx_ref, lq1_ref, lk1_ref, lq2_ref, lk2_ref,
                 subg_ref, o_ref, qx_ref, m_ref, acc_ref, *, tq, lam_init):
    _attend_all(q_ref, k_lat_ref, k_ctx_ref, vt_lat_ref, vt_ctx_ref, qx_ref, m_ref, acc_ref,
                lambda i: (i // 2) * LANES, lambda i: i // 2, tq)
    lam = _diff_lambda(lq1_ref, lk1_ref, lq2_ref, lk2_ref, lam_init)
    for head in range(C_HEADS):
        outs = [acc_ref[2 * head + h, 0:C_V_DIM] / acc_ref[2 * head + h, C_V_DIM:C_V_DIM + 1] for h in range(2)]
        o_ref[:, head * LANES:(head + 1) * LANES] = _diff_combine(
            outs[0], outs[1], lam, subg_ref[...], lam_init).astype(BF16)


def _attention_call(kernel, name, q, k_lat, k_ctx, vt_lat, vt_ctx, extra, *, tokens_per_batch, tq):
    nq = tokens_per_batch // tq
    kw = k_lat.shape[-1]
    g, dvx = vt_lat.shape[1], vt_lat.shape[3]
    bmap3 = lambda b, i: (b, 0, 0)
    bmap5 = lambda b, i: (b, 0, 0, 0, 0)
    in_specs = [
        pl.BlockSpec((tq, q.shape[1]), lambda b, i: (b * nq + i, 0)),
        pl.BlockSpec((1, k_lat.shape[1], kw), bmap3),
        pl.BlockSpec((1, k_ctx.shape[1], kw), bmap3),
        pl.BlockSpec((1, g, vt_lat.shape[2], dvx, vt_lat.shape[4]), bmap5),
        pl.BlockSpec((1, g, 1, dvx, vt_ctx.shape[4]), bmap5),
    ] + [_resident(e.shape) for e in extra]
    return pl.pallas_call(
        kernel,
        grid=(BATCH, nq),
        in_specs=in_specs,
        out_specs=pl.BlockSpec((tq, q.shape[1]), lambda b, i: (b * nq + i, 0)),
        out_shape=jax.ShapeDtypeStruct(q.shape, BF16),
        scratch_shapes=[pltpu.VMEM((N_PROBLEMS, tq, LANES), BF16), pltpu.VMEM((N_PROBLEMS, 1, tq), F32),
                        pltpu.VMEM((N_PROBLEMS, dvx, tq), F32)],
        compiler_params=_cparams("parallel", "arbitrary"),
        name=name,
    )(q, k_lat, k_ctx, vt_lat, vt_ctx, *extra)


def _attend_ctx(q_ext, k, vt):
    dn = (((1,), (1,)), ((), ()))
    st = lax.dot_general(k, q_ext, dn, preferred_element_type=F32)
    p = jnp.exp2(st - jnp.max(st, axis=0, keepdims=True)).astype(BF16)
    return _dot(vt, p)


def _gqa_ctx_kernel(q_ref, k_ref, vt_ref, o_ref, *, tq):
    for pair in range(A_GROUP):
        outs = []
        for half, q_ext in enumerate(_half_queries(q_ref, pair, tq)):
            acc = _attend_ctx(q_ext, k_ref[0], vt_ref[0, half, 0])
            outs.append(acc[0:HEAD_DIM] / acc[HEAD_DIM:HEAD_DIM + 1])
        o_ref[:, pair * LANES:(pair + 1) * LANES] = jnp.concatenate(outs, axis=0).T.astype(BF16)


def _diff_ctx_kernel(q_ref, k_ref, vt_ref, lq1_ref, lk1_ref, lq2_ref, lk2_ref, subg_ref, o_ref, *, tq, lam_init):
    lam = _diff_lambda(lq1_ref, lk1_ref, lq2_ref, lk2_ref, lam_init)
    for head in range(C_HEADS):
        outs = []
        for q_ext in _half_queries(q_ref, head, tq):
            acc = _attend_ctx(q_ext, k_ref[0, :, head * LANES:(head + 1) * LANES], vt_ref[0, head, 0])
            outs.append(acc[0:C_V_DIM] / acc[C_V_DIM:C_V_DIM + 1])
        o_ref[:, head * LANES:(head + 1) * LANES] = _diff_combine(
            outs[0], outs[1], lam, subg_ref[...], lam_init).astype(BF16)


def _ctx_attention_call(kernel, name, q, k, vt, extra):
    tq = CTX_LEN
    g, dvx = vt.shape[1], vt.shape[3]
    in_specs = [
        pl.BlockSpec((tq, q.shape[1]), lambda b: (b, 0)),
        pl.BlockSpec((1, tq, k.shape[-1]), lambda b: (b, 0, 0)),
        pl.BlockSpec((1, g, 1, dvx, tq), lambda b: (b, 0, 0, 0, 0)),
    ] + [_resident(e.shape) for e in extra]
    return pl.pallas_call(
        kernel,
        grid=(BATCH,),
        in_specs=in_specs,
        out_specs=pl.BlockSpec((tq, q.shape[1]), lambda b: (b, 0)),
        out_shape=jax.ShapeDtypeStruct(q.shape, BF16),
        compiler_params=_cparams("parallel"),
        name=name,
    )(q, k, vt, *extra)


CONV_HALO = 16


SUBLANES = 8
CONV_ROWS = 32


def _conv_kernel(prev_ref, cur_ref, next_ref, w_ref, b_ref, g_ref, beta_ref, o_ref, win_ref, sh_ref, *, tc, nblk):
    j = pl.program_id(1)
    win_ref[0:CONV_HALO, :] = jnp.where(j > 0, prev_ref[0], 0.0)
    win_ref[CONV_HALO:CONV_HALO + tc, :] = cur_ref[0]
    win_ref[CONV_HALO + tc:CONV_HALO + tc + CONV_HALO, :] = jnp.where(j < nblk - 1, next_ref[0], 0.0)
    span = sh_ref.shape[1]
    for s in range(1, SUBLANES):
        sh_ref[s - 1] = win_ref[s:s + span, :]
    first = CONV_HALO - B_CONV_W // 2
    nsub = CONV_ROWS // SUBLANES
    for r in range(0, tc, CONV_ROWS):
        accs = [jnp.zeros((SUBLANES, B_CHANNELS), F32) + b_ref[...]] * nsub
        for k in range(B_CONV_W):
            a, s = divmod(first + k, SUBLANES)
            wk = w_ref[k]
            for q in range(nsub):
                lo = r + (a + q) * SUBLANES
                src = win_ref[lo:lo + SUBLANES, :] if s == 0 else sh_ref[s - 1, lo:lo + SUBLANES, :]
                accs[q] = accs[q] + src * wk
        acc = jnp.concatenate(accs, axis=0)
        mu = jnp.mean(acc, axis=-1, keepdims=True)
        cen = acc - mu
        var = jnp.mean(cen * cen, axis=-1, keepdims=True)
        y = cen * lax.rsqrt(var + EPS) * g_ref[...] + beta_ref[...]
        o_ref[0, r:r + CONV_ROWS, :] = _silu(y).astype(BF16)


def _conv_call(yb, dw_w, dw_b, ln_g, ln_b, *, tokens_per_batch):
    p = tokens_per_batch
    tc = 256
    nblk = p // tc
    r = tc // CONV_HALO
    nh = p // CONV_HALO
    y3 = yb.reshape(BATCH, p, B_CHANNELS)
    out = pl.pallas_call(
        functools.partial(_conv_kernel, tc=tc, nblk=nblk),
        grid=(BATCH, nblk),
        in_specs=[
            pl.BlockSpec((1, CONV_HALO, B_CHANNELS), lambda b, j: (b, jnp.maximum(j * r - 1, 0), 0)),
            pl.BlockSpec((1, tc, B_CHANNELS), lambda b, j: (b, j, 0)),
            pl.BlockSpec((1, CONV_HALO, B_CHANNELS), lambda b, j: (b, jnp.minimum((j + 1) * r, nh - 1), 0)),
            _resident(dw_w.shape),
            _resident((1, B_CHANNELS)),
            _resident((1, B_CHANNELS)),
            _resident((1, B_CHANNELS)),
        ],
        out_specs=pl.BlockSpec((1, tc, B_CHANNELS), lambda b, j: (b, j, 0)),
        out_shape=jax.ShapeDtypeStruct((BATCH, p, B_CHANNELS), BF16),
        scratch_shapes=[pltpu.VMEM((tc + 2 * CONV_HALO, B_CHANNELS), F32),
                        pltpu.VMEM((SUBLANES - 1, tc + 2 * CONV_HALO - SUBLANES, B_CHANNELS), F32)],
        compiler_params=_cparams("parallel", "arbitrary"),
        name="conformer_conv",
    )(y3, y3, y3, dw_w, dw_b, ln_g, ln_b)
    return out.reshape(BATCH * p, B_CHANNELS)


def _merge_kernel(x_ref, mod_ref, oa_ref, ob_ref, oc_ref, gt_ref, wpa_ref, wpb_ref, wpc_ref, wout_ref, o_ref):
    d = D_MODEL
    m = (gt_ref[:, 0:d].astype(F32) * _dot(oa_ref[...], wpa_ref[...])
         + gt_ref[:, d:2 * d].astype(F32) * _dot(ob_ref[...], wpb_ref[...])
         + gt_ref[:, 2 * d:3 * d].astype(F32) * _dot(oc_ref[...], wpc_ref[...]))
    y = _dot(m.astype(BF16), wout_ref[...])
    gate1 = mod_ref[0][:, 2 * d:3 * d]
    o_ref[...] = x_ref[...] + gate1 * y


def _merge_call(x, mod3, mod_row, oa, ob, oc, gt, wpa, wpb, wpc, wout, *, tm):
    t = x.shape[0]
    row = lambda i: (i, 0)
    return pl.pallas_call(
        _merge_kernel,
        grid=(t // tm,),
        in_specs=[
            pl.BlockSpec((tm, D_MODEL), row),
            pl.BlockSpec((1, 1, 6 * D_MODEL), lambda i: (mod_row(i), 0, 0)),
            pl.BlockSpec((tm, A_Q), row),
            pl.BlockSpec((tm, B_CHANNELS), row),
            pl.BlockSpec((tm, C_V), row),
            pl.BlockSpec((tm, N_BRANCH * D_MODEL), row),
            _resident(wpa.shape), _resident(wpb.shape), _resident(wpc.shape), _resident(wout.shape),
        ],
        out_specs=pl.BlockSpec((tm, D_MODEL), row),
        out_shape=jax.ShapeDtypeStruct(x.shape, F32),
        compiler_params=_cparams("parallel"),
        name="branch_merge",
    )(x, mod3, oa, ob, oc, gt, wpa, wpb, wpc, wout)


FFN_SPLIT = 2


def _ffn_kernel(x_ref, mod_ref, g2_ref, w1_ref, w3_ref, w2_ref, o_ref):
    d = D_MODEL
    mod = mod_ref[0]
    x = x_ref[...]
    h = _norm_mod(x, g2_ref[...], mod[:, 3 * d:4 * d], mod[:, 4 * d:5 * d]).astype(BF16)
    fc = D_FF // FFN_SPLIT
    y = jnp.zeros(x.shape, F32)
    for s in range(FFN_SPLIT):
        a = _dot(h, w1_ref[:, s * fc:(s + 1) * fc])
        b = _dot(h, w3_ref[:, s * fc:(s + 1) * fc])
        y = y + _dot((_silu(a) * b).astype(BF16), w2_ref[s * fc:(s + 1) * fc, :])
    o_ref[...] = x + mod[:, 5 * d:6 * d] * y


def _ffn_call(x, mod3, mod_row, g2, w1, w3, w2, *, tm):
    t = x.shape[0]
    row = lambda i: (i, 0)
    return pl.pallas_call(
        _ffn_kernel,
        grid=(t // tm,),
        in_specs=[
            pl.BlockSpec((tm, D_MODEL), row),
            pl.BlockSpec((1, 1, 6 * D_MODEL), lambda i: (mod_row(i), 0, 0)),
            _resident((1, D_MODEL)),
            _resident(w1.shape), _resident(w3.shape), _resident(w2.shape),
        ],
        out_specs=pl.BlockSpec((tm, D_MODEL), row),
        out_shape=jax.ShapeDtypeStruct(x.shape, F32),
        compiler_params=_cparams("parallel"),
        name="dense_swiglu",
    )(x, mod3, g2, w1, w3, w2)


MOE_F_TILE = 1792
MOE_TM = 512
TOP_K = 2


def _router_kernel(x_ref, mod_ref, g2_ref, wr_ref, br_ref, h_ref, sel_ref, wts_ref):
    d = D_MODEL
    mod = mod_ref[0]
    h = _norm_mod(x_ref[...], g2_ref[...], mod[:, 3 * d:4 * d], mod[:, 4 * d:5 * d])
    h_ref[...] = h
    logits = jnp.dot(h, wr_ref[...], preferred_element_type=F32, precision=lax.Precision.HIGHEST) + br_ref[...]
    lane = lax.broadcasted_iota(jnp.int32, logits.shape, 1)
    v1 = jnp.max(logits, axis=-1, keepdims=True)
    i1 = jnp.min(jnp.where(logits == v1, lane, LANES), axis=-1, keepdims=True)
    rest = jnp.where(lane == i1, -jnp.inf, logits)
    v2 = jnp.max(rest, axis=-1, keepdims=True)
    i2 = jnp.min(jnp.where(rest == v2, lane, LANES), axis=-1, keepdims=True)
    e2 = jnp.exp(v2 - v1)
    w1 = 1.0 / (1.0 + e2)
    sel_ref[...] = jnp.where(lane == 0, i1, jnp.where(lane == 1, i2, 0))
    wts_ref[...] = jnp.where(lane == 0, w1, jnp.where(lane == 1, e2 * w1, 0.0))


def _router_call(x, mod3, mod_row, g2, wr, br, *, tm):
    t = x.shape[0]
    row = lambda i: (i, 0)
    return pl.pallas_call(
        _router_kernel,
        grid=(t // tm,),
        in_specs=[
            pl.BlockSpec((tm, D_MODEL), row),
            pl.BlockSpec((1, 1, 6 * D_MODEL), lambda i: (mod_row(i), 0, 0)),
            _resident((1, D_MODEL)), _resident((D_MODEL, LANES)), _resident((1, LANES)),
        ],
        out_specs=(pl.BlockSpec((tm, D_MODEL), row), pl.BlockSpec((tm, LANES), row), pl.BlockSpec((tm, LANES), row)),
        out_shape=(jax.ShapeDtypeStruct((t, D_MODEL), F32), jax.ShapeDtypeStruct((t, LANES), jnp.int32),
                   jax.ShapeDtypeStruct((t, LANES), F32)),
        compiler_params=_cparams("parallel"),
        name="moe_router",
    )(x, mod3, g2, wr, br)


def _dispatch_plan(sel, n_rows):
    e_flat = sel.reshape(-1)
    n_assign = e_flat.shape[0]
    onehot = (e_flat[:, None] == jnp.arange(N_EXPERTS, dtype=jnp.int32)[None, :]).astype(jnp.int32)
    csum = jnp.cumsum(onehot, axis=0)
    rank = jnp.take_along_axis(csum, e_flat[:, None], axis=1)[:, 0] - 1
    padded = (csum[-1] + MOE_TM - 1) // MOE_TM * MOE_TM
    end = jnp.cumsum(padded)
    pos = (end - padded)[e_flat] + rank
    row_token = jnp.zeros((n_rows,), jnp.int32).at[pos].set(
        jnp.arange(n_assign, dtype=jnp.int32) // TOP_K, unique_indices=True, mode="promise_in_bounds")
    tile_start = jnp.arange(n_rows // MOE_TM, dtype=jnp.int32) * MOE_TM
    tile_expert = jnp.minimum(jnp.sum(tile_start[:, None] >= end[None, :], axis=1), N_EXPERTS - 1).astype(jnp.int32)
    n_tiles = (end[-1:] // MOE_TM).astype(jnp.int32)
    return row_token, pos.astype(jnp.int32), tile_expert, n_tiles


def _gather_rows(idx_ref, idx0, src_hbm, dst_ref, sem, row0, n, inline=False):
    def body(r, carry):
        src = src_hbm.at[pl.ds(idx_ref[idx0 + r], 1)]
        pltpu.make_async_copy(src, dst_ref.at[pl.ds(row0 + r, 1)], sem).start()
        return carry
    if inline:
        for r in range(n):
            body(r, 0)
    else:
        lax.fori_loop(0, n, body, 0, unroll=8)


def _wait_rows(src_hbm, dst_ref, sem):
    pltpu.make_async_copy(src_hbm.at[pl.ds(0, dst_ref.shape[0])], dst_ref, sem).wait()


def _expert_kernel(row_token_ref, tile_expert_ref, n_tiles_ref, h_hbm, w1_ref, w3_ref, w2_ref, y_ref,
                   xbuf_ref, acc_ref, sem_ref):
    j = pl.program_id(0)
    f = pl.program_id(1)
    nf = pl.num_programs(1)
    slot = j % 2
    valid = j < n_tiles_ref[0]

    @pl.when(jnp.logical_and(j == 0, f == 0))
    def _():
        _gather_rows(row_token_ref, 0, h_hbm, xbuf_ref.at[0], sem_ref.at[0], 0, MOE_TM)

    @pl.when(jnp.logical_and(valid, f == 0))
    def _():
        _wait_rows(h_hbm, xbuf_ref.at[slot], sem_ref.at[slot])

    def compute(gather_next):
        if gather_next:
            share = MOE_TM // (D_FF_EXPERT // MOE_F_TILE)
            _gather_rows(row_token_ref, (j + 1) * MOE_TM + f * share, h_hbm, xbuf_ref.at[1 - slot],
                         sem_ref.at[1 - slot], f * share, share, inline=True)
        x = xbuf_ref[slot].astype(BF16)
        z = (_silu(_dot(x, w1_ref[0])) * _dot(x, w3_ref[0])).astype(BF16)
        part = _dot(z, w2_ref[0])

        @pl.when(f == 0)
        def _():
            acc_ref[...] = part

        @pl.when(f > 0)
        def _():
            acc_ref[...] += part

        @pl.when(f == nf - 1)
        def _():
            y_ref[...] = acc_ref[...]

    has_next = j + 1 < n_tiles_ref[0]
    pl.when(has_next)(functools.partial(compute, True))
    pl.when(jnp.logical_and(valid, jnp.logical_not(has_next)))(functools.partial(compute, False))

    @pl.when(jnp.logical_and(jnp.logical_not(valid), f == nf - 1))
    def _():
        y_ref[...] = jnp.zeros(y_ref.shape, F32)


def _expert_call(h, row_token, tile_expert, n_tiles, w1, w3, w2):
    n_rows = row_token.shape[0]
    nt = n_rows // MOE_TM
    nf = D_FF_EXPERT // MOE_F_TILE

    def f_eff(j, f, n):
        return jnp.where(j < n[0], f, nf - 1)

    grid_spec = pltpu.PrefetchScalarGridSpec(
        num_scalar_prefetch=3,
        grid=(nt, nf),
        in_specs=[
            pl.BlockSpec(memory_space=pl.ANY),
            pl.BlockSpec((1, D_MODEL, MOE_F_TILE), lambda j, f, rt, te, n: (te[j], 0, f_eff(j, f, n))),
            pl.BlockSpec((1, D_MODEL, MOE_F_TILE), lambda j, f, rt, te, n: (te[j], 0, f_eff(j, f, n))),
            pl.BlockSpec((1, MOE_F_TILE, D_MODEL), lambda j, f, rt, te, n: (te[j], f_eff(j, f, n), 0)),
        ],
        out_specs=pl.BlockSpec((MOE_TM, D_MODEL), lambda j, f, rt, te, n: (j, 0)),
        scratch_shapes=[pltpu.VMEM((2, MOE_TM, D_MODEL), F32), pltpu.VMEM((MOE_TM, D_MODEL), F32),
                        pltpu.SemaphoreType.DMA((2,))],
    )
    return pl.pallas_call(
        _expert_kernel,
        grid_spec=grid_spec,
        out_shape=jax.ShapeDtypeStruct((n_rows, D_MODEL), F32),
        compiler_params=_cparams("arbitrary", "arbitrary"),
        name="moe_experts",
    )(row_token, tile_expert, n_tiles, h, w1, w3, w2)


def _combine_kernel(pos_ref, y_hbm, x_ref, mod_ref, wts_ref, fg_ref, o_ref, ybuf_ref, sem_ref, *, tc, final_norm):
    i = pl.program_id(0)
    slot = i % 2
    rows = TOP_K * tc

    @pl.when(i == 0)
    def _():
        _gather_rows(pos_ref, 0, y_hbm, ybuf_ref.at[0], sem_ref.at[0], 0, rows)

    @pl.when(i + 1 < pl.num_programs(0))
    def _():
        _gather_rows(pos_ref, (i + 1) * rows, y_hbm, ybuf_ref.at[1 - slot], sem_ref.at[1 - slot], 0, rows,
                     inline=True)

    _wait_rows(y_hbm, ybuf_ref.at[slot], sem_ref.at[slot])
    wts = wts_ref[...]
    moe = wts[:, 0:1] * ybuf_ref[slot, 0:tc, :] + wts[:, 1:2] * ybuf_ref[slot, tc:rows, :]
    y = x_ref[...] + mod_ref[0][:, 5 * D_MODEL:6 * D_MODEL] * moe
    if final_norm:
        y = y * lax.rsqrt(jnp.mean(y * y, axis=-1, keepdims=True) + EPS) * fg_ref[...]
    o_ref[...] = y


def _combine_call(x, mod3, mod_row, y, pos, wts, fg, *, tc, final_norm):
    t = x.shape[0]
    row = lambda i, p: (i, 0)
    pos_tiles = pos.reshape(t // tc, tc, TOP_K).transpose(0, 2, 1).reshape(-1)
    grid_spec = pltpu.PrefetchScalarGridSpec(
        num_scalar_prefetch=1,
        grid=(t // tc,),
        in_specs=[
            pl.BlockSpec(memory_space=pl.ANY),
            pl.BlockSpec((tc, D_MODEL), row),
            pl.BlockSpec((1, 1, 6 * D_MODEL), lambda i, p: (mod_row(i), 0, 0)),
            pl.BlockSpec((tc, LANES), row),
            pl.BlockSpec((1, D_MODEL), lambda i, p: (0, 0)),
        ],
        out_specs=pl.BlockSpec((tc, D_MODEL), row),
        scratch_shapes=[pltpu.VMEM((2, TOP_K * tc, D_MODEL), F32), pltpu.SemaphoreType.DMA((2,))],
    )
    return pl.pallas_call(
        functools.partial(_combine_kernel, tc=tc, final_norm=final_norm),
        grid_spec=grid_spec,
        out_shape=jax.ShapeDtypeStruct(x.shape, F32),
        compiler_params=_cparams("arbitrary"),
        name="moe_combine",
    )(pos_tiles, y, x, mod3, wts, fg)


def _moe_call(x, mod3, mod_row, g2, wr, br, w1, w3, w2, fg, *, tm, final_norm):
    t = x.shape[0]
    h, sel, wts = _router_call(x, mod3, mod_row, g2, wr, br, tm=tm)
    n_rows = TOP_K * t + N_EXPERTS * MOE_TM
    row_token, pos, tile_expert, n_tiles = _dispatch_plan(sel[:, :TOP_K], n_rows)
    y = _expert_call(h, row_token, tile_expert, n_tiles, w1, w3, w2)
    tc = 256
    return _combine_call(x, mod3, lambda i: mod_row(i * tc // tm), y, pos, wts, fg, tc=tc, final_norm=final_norm)


def _final_norm_kernel(x_ref, g_ref, o_ref):
    x = x_ref[...]
    o_ref[...] = x * lax.rsqrt(jnp.mean(x * x, axis=-1, keepdims=True) + EPS) * g_ref[...]


def _final_norm_call(x, g, *, tm):
    row = lambda i: (i, 0)
    return pl.pallas_call(
        _final_norm_kernel,
        grid=(x.shape[0] // tm,),
        in_specs=[pl.BlockSpec((tm, D_MODEL), row), _resident((1, D_MODEL))],
        out_specs=pl.BlockSpec((tm, D_MODEL), row),
        out_shape=jax.ShapeDtypeStruct(x.shape, F32),
        compiler_params=_cparams("parallel"),
        name="final_rmsnorm",
    )(x, g)


def _rope_tables():
    rows = SEQ // GRID_W
    row = jnp.repeat(jnp.arange(rows), GRID_W)
    col = jnp.tile(jnp.arange(GRID_W), rows)
    n_freq = HEAD_DIM // 4
    inv = ROPE_THETA ** (-jnp.arange(n_freq, dtype=F32) / n_freq)
    ang = jnp.stack([row, col], axis=-1).astype(F32)[:, :, None] * inv
    cos, sin = jnp.cos(ang), jnp.sin(ang)
    cos64 = jnp.concatenate([cos[:, 0], cos[:, 0], cos[:, 1], cos[:, 1]], axis=-1)
    sin64 = jnp.concatenate([-sin[:, 0], sin[:, 0], -sin[:, 1], sin[:, 1]], axis=-1)
    return jnp.tile(cos64, (1, 2)), jnp.tile(sin64, (1, 2))


def _pair_heads(w, axis):
    shp = w.shape
    w = w.reshape(shp[:axis] + (A_KV_HEADS, A_GROUP, HEAD_DIM) + shp[axis + 1:])
    w = jnp.swapaxes(w, axis, axis + 1)
    return w.reshape(shp)


def kernel(x, c, ctx, c_ctx, w_mod, b_mod, norm1_g, norm2_g, w_in, b_gate, a_qn_g, a_kn_g, b_dw_w, b_dw_b,
           b_ln_g, b_ln_b, c_lq1, c_lk1, c_lq2, c_lk2, c_subln_g, w_pa, w_pb, w_pc, w_out, ffn_w1, ffn_w3,
           ffn_w2, moe_router, moe_router_b, moe_w1, moe_w3, moe_w2, final_g):
    d = D_MODEL
    t_lat = BATCH * SEQ
    xl = x.reshape(t_lat, d)
    xc = ctx.reshape(BATCH * CTX_LEN, d)

    cc = jnp.concatenate([c, c_ctx[None, :], jnp.zeros((3, d), F32)], axis=0)
    mod_all = _mod_call(cc, w_mod, b_mod)

    cos_t, sin_t = _rope_tables()
    head_id = np.arange(A_Q) // HEAD_DIM
    bd = jnp.asarray(head_id[:, None] == head_id[None, :], BF16)

    lat_tiles_per_batch = SEQ // KEY_CHUNK
    lat_row = lambda i: i // lat_tiles_per_batch
    ctx_row = lambda i: 4
    tm = KEY_CHUNK

    for i in range(DEPTH):
        last = i == DEPTH - 1
        lam_init = 0.8 - 0.6 * math.exp(-0.3 * i)
        mod3 = mod_all[i].reshape(8, 1, 6 * d)
        g1 = norm1_g[i].reshape(1, d)
        g2 = norm2_g[i].reshape(1, d)
        w_in_i = jnp.concatenate([_pair_heads(w_in[i][:, :A_Q], 1), w_in[i][:, A_Q:]], axis=1).astype(BF16)
        bg = b_gate[i].reshape(1, -1)
        qn = jnp.tile(a_qn_g[i], A_HEADS).reshape(1, A_Q)
        kn = jnp.tile(a_kn_g[i], A_KV_HEADS).reshape(1, A_KV)
        lam_vecs = [v[i].reshape(1, HEAD_DIM) for v in (c_lq1, c_lk1, c_lq2, c_lk2)]
        subg = c_subln_g[i].reshape(1, C_V_DIM)
        wpa = _pair_heads(w_pa[i], 0).astype(BF16)
        wpb, wpc, wout = w_pb[i].astype(BF16), w_pc[i].astype(BF16), w_out[i].astype(BF16)
        dw_w = jnp.broadcast_to(b_dw_w[i][:, None, :], (B_CONV_W, SUBLANES, B_CHANNELS))
        conv_p = (dw_w, b_dw_b[i].reshape(1, -1), b_ln_g[i].reshape(1, -1), b_ln_b[i].reshape(1, -1))

        proj = functools.partial(_inproj_call, g1=g1, w_in=w_in_i, b_gate=bg, qn=qn, kn=kn, bd=bd,
                                 cos_t=cos_t, sin_t=sin_t)
        qa, ka, vta, yb, qc, kc, vtc, gt = proj(xl, mod3, lat_row, tokens_per_batch=SEQ, rope=True)
        qa_c, ka_c, vta_c, yb_c, qc_c, kc_c, vtc_c, gt_c = proj(xc, mod3, ctx_row, tokens_per_batch=CTX_LEN,
                                                                rope=False)
        ka3, kc3 = ka.reshape(BATCH, SEQ, A_KV), kc.reshape(BATCH, SEQ, C_QK)
        ka3_c, kc3_c = ka_c.reshape(BATCH, CTX_LEN, A_KV), kc_c.reshape(BATCH, CTX_LEN, C_QK)

        oa = _attention_call(functools.partial(_gqa_kernel, tq=256), "gqa_attention", qa, ka3, ka3_c, vta, vta_c,
                             [], tokens_per_batch=SEQ, tq=256)
        oc = _attention_call(functools.partial(_diff_kernel, tq=256, lam_init=lam_init), "diff_attention",
                             qc, kc3, kc3_c, vtc, vtc_c, lam_vecs + [subg], tokens_per_batch=SEQ, tq=256)
        ob = _conv_call(yb, *conv_p, tokens_per_batch=SEQ)
        xl = _merge_call(xl, mod3, lat_row, oa, ob, oc, gt, wpa, wpb, wpc, wout, tm=tm)

        if not last:
            oa_c = _ctx_attention_call(functools.partial(_gqa_ctx_kernel, tq=CTX_LEN), "gqa_attention_ctx",
                                       qa_c, ka3_c, vta_c, [])
            oc_c = _ctx_attention_call(functools.partial(_diff_ctx_kernel, tq=CTX_LEN, lam_init=lam_init),
                                       "diff_attention_ctx", qc_c, kc3_c, vtc_c, lam_vecs + [subg])
            ob_c = _conv_call(yb_c, *conv_p, tokens_per_batch=CTX_LEN)
            xc = _merge_call(xc, mod3, ctx_row, oa_c, ob_c, oc_c, gt_c, wpa, wpb, wpc, wout, tm=tm)

        j = i // 2
        if i % 2 == 0:
            ffn_w = (ffn_w1[j].astype(BF16), ffn_w3[j].astype(BF16), ffn_w2[j].astype(BF16))
            xl = _ffn_call(xl, mod3, lat_row, g2, *ffn_w, tm=tm)
            if last:
                xl = _final_norm_call(xl, final_g.reshape(1, d), tm=tm)
            else:
                xc = _ffn_call(xc, mod3, ctx_row, g2, *ffn_w, tm=tm)
        else:
            wr = jnp.zeros((d, LANES), F32).at[:, :N_EXPERTS].set(moe_router[j])
            br = jnp.full((1, LANES), -jnp.inf, F32).at[0, :N_EXPERTS].set(moe_router_b[j])
            moe_w = (moe_w1[j].astype(BF16), moe_w3[j].astype(BF16), moe_w2[j].astype(BF16))
            fg = final_g.reshape(1, d)
            xl = _moe_call(xl, mod3, lat_row, g2, wr, br, *moe_w, fg, tm=tm, final_norm=last)
            if not last:
                xc = _moe_call(xc, mod3, ctx_row, g2, wr, br, *moe_w, fg, tm=tm, final_norm=False)
    return xl.reshape(BATCH, SEQ, d)
```

```python
import functools
import math

import numpy as np
import jax
import jax.numpy as jnp
from jax import lax
from jax.experimental import pallas as pl
from jax.experimental.pallas import tpu as pltpu

F32 = jnp.float32
BF16 = jnp.bfloat16

D_MODEL = 1024
BATCH = 4
SEQ = 4096
DEPTH = 2
GRID_W = 64
CTX_LEN = 256
EPS = 1e-6
ROPE_THETA = 10000.0

HEAD_DIM = 64
A_HEADS = 8
A_KV_HEADS = 2
A_GROUP = 4
A_Q = 512
A_KV = 128
B_CHANNELS = 512
B_CONV_W = 31
C_HEADS = 4
C_V_DIM = 128
C_QK = 512
C_V = 512
N_BRANCH = 3
IN_SPLITS = [A_Q, A_KV, A_KV, 2 * B_CHANNELS, C_QK, C_QK, C_V, N_BRANCH * D_MODEL]
IN_WIDTH = sum(IN_SPLITS)
IN_OFF = [0] + [int(v) for v in np.cumsum(IN_SPLITS)]
D_FF = 2816
N_EXPERTS = 8
D_FF_EXPERT = 3584

Q_SCALE = HEAD_DIM ** -0.5 * 1.4426950408889634

LANES = 128
ONES_ROWS = 16
KEY_CHUNK = 512
VMEM_LIMIT = 56 * 1024 * 1024


def _cparams(*sem):
    return pltpu.CompilerParams(dimension_semantics=sem, vmem_limit_bytes=VMEM_LIMIT)


def _resident(shape):
    nd = len(shape)
    return pl.BlockSpec(shape, lambda *_: (0,) * nd, pipeline_mode=pl.Buffered(1))


def _sigmoid(v):
    return 1.0 / (1.0 + jnp.exp(-v))


def _silu(v):
    return v * _sigmoid(v)


def _norm_mod(x, g, shift, scale):
    ms = jnp.mean(x * x, axis=-1, keepdims=True)
    return (x * lax.rsqrt(ms + EPS) * g) * (1.0 + scale) + shift


def _dot(a, b):
    return jnp.dot(a, b, preferred_element_type=F32)


def _mod_kernel(c_ref, w_ref, b_ref, o_ref):
    s = _silu(c_ref[...])
    o_ref[0] = jnp.dot(s, w_ref[0], preferred_element_type=F32, precision=lax.Precision.HIGHEST) + b_ref[0]


def _mod_call(cc, w_mod, b_mod):
    n = 6 * D_MODEL
    tn = D_MODEL
    return pl.pallas_call(
        _mod_kernel,
        grid=(DEPTH, n // tn),
        in_specs=[
            pl.BlockSpec((8, D_MODEL), lambda i, j: (0, 0)),
            pl.BlockSpec((1, D_MODEL, tn), lambda i, j: (i, 0, j)),
            pl.BlockSpec((1, 1, tn), lambda i, j: (i, 0, j)),
        ],
        out_specs=pl.BlockSpec((1, 8, tn), lambda i, j: (i, 0, j)),
        out_shape=jax.ShapeDtypeStruct((DEPTH, 8, n), F32),
        compiler_params=_cparams("arbitrary", "arbitrary"),
        name="adaln_mod",
    )(cc, w_mod, b_mod.reshape(DEPTH, 1, n))


def _segment_mean_sq(u, bd):
    sq = u * u
    hi = sq.astype(BF16)
    lo = (sq - hi.astype(F32)).astype(BF16)
    return (_dot(hi, bd) + _dot(lo, bd)) * (1.0 / HEAD_DIM)


def _rope_group(xg, cos, sin, lo_mask):
    partner = jnp.where(lo_mask, pltpu.roll(xg, LANES - 16, 1), pltpu.roll(xg, 16, 1))
    return xg * cos + partner * sin


def _inproj_kernel(x_ref, mod_ref, g1_ref, w_ref, bgate_ref, qn_ref, kn_ref, bd_ref, cos_ref, sin_ref,
                   qa_ref, ka_ref, vta_ref, yb_ref, qc_ref, kc_ref, vtc_ref, gt_ref, *, rope, tm):
    mod = mod_ref[0]
    h = _norm_mod(x_ref[...], g1_ref[...], mod[:, 0:D_MODEL], mod[:, D_MODEL:2 * D_MODEL]).astype(BF16)

    def proj(c0, c1):
        return _dot(h, w_ref[:, c0:c1])

    if rope:
        cos = cos_ref[...]
        sin = sin_ref[...]
        lane = lax.broadcasted_iota(jnp.int32, (tm, LANES), 1)
        lo_mask = (lane % 32) < 16

    def rot(u, j):
        ug = u[:, j * LANES:(j + 1) * LANES]
        return _rope_group(ug, cos, sin, lo_mask) if rope else ug

    ones_pat = jnp.where(lax.broadcasted_iota(jnp.int32, (ONES_ROWS, tm), 0) == 0, 1.0, 0.0).astype(BF16)

    u = proj(IN_OFF[0], IN_OFF[1])
    u = u * lax.rsqrt(_segment_mean_sq(u, bd_ref[...]) + EPS) * (qn_ref[...] * Q_SCALE)
    for j in range(A_Q // LANES):
        qa_ref[:, j * LANES:(j + 1) * LANES] = rot(u, j).astype(BF16)
    u = proj(IN_OFF[1], IN_OFF[2])
    u = u * lax.rsqrt(_segment_mean_sq(u, bd_ref[0:A_KV, 0:A_KV]) + EPS) * kn_ref[...]
    ka_ref[0] = rot(u, 0).astype(BF16)
    ut = proj(IN_OFF[2], IN_OFF[3]).T
    for j in range(A_KV_HEADS):
        vta_ref[0, j, 0, 0:HEAD_DIM, :] = ut[j * HEAD_DIM:(j + 1) * HEAD_DIM].astype(BF16)
        vta_ref[0, j, 0, HEAD_DIM:HEAD_DIM + ONES_ROWS, :] = ones_pat
    a = proj(IN_OFF[3], IN_OFF[3] + B_CHANNELS)
    g = proj(IN_OFF[3] + B_CHANNELS, IN_OFF[4])
    yb_ref[...] = a * _sigmoid(g)
    u = proj(IN_OFF[4], IN_OFF[5]) * Q_SCALE
    for j in range(C_QK // LANES):
        qc_ref[:, j * LANES:(j + 1) * LANES] = rot(u, j).astype(BF16)
    u = proj(IN_OFF[5], IN_OFF[6])
    for j in range(C_QK // LANES):
        kc_ref[0, :, j * LANES:(j + 1) * LANES] = rot(u, j).astype(BF16)
    ut = proj(IN_OFF[6], IN_OFF[7]).T
    for j in range(C_HEADS):
        vtc_ref[0, j, 0, 0:C_V_DIM, :] = ut[j * C_V_DIM:(j + 1) * C_V_DIM].astype(BF16)
        vtc_ref[0, j, 0, C_V_DIM:C_V_DIM + ONES_ROWS, :] = ones_pat
    for j in range(N_BRANCH):
        c0 = IN_OFF[7] + j * D_MODEL
        gt_ref[:, j * D_MODEL:(j + 1) * D_MODEL] = _sigmoid(
            proj(c0, c0 + D_MODEL) + bgate_ref[:, j * D_MODEL:(j + 1) * D_MODEL]).astype(BF16)


def _inproj_call(x, mod3, mod_row, g1, w_in, b_gate, qn, kn, bd, cos_t, sin_t, *, tokens_per_batch, rope):
    p = tokens_per_batch
    tm = min(p, KEY_CHUNK)
    nblk = p // tm
    t = BATCH * p
    row = lambda i: (i, 0)
    vt_map = lambda i: (i // nblk, 0, i % nblk, 0, 0)
    key_map = lambda i: (i // nblk, i % nblk, 0)
    out_shape = (
        jax.ShapeDtypeStruct((t, A_Q), BF16),
        jax.ShapeDtypeStruct((BATCH, p, A_KV), BF16),
        jax.ShapeDtypeStruct((BATCH, A_KV_HEADS, nblk, HEAD_DIM + ONES_ROWS, tm), BF16),
        jax.ShapeDtypeStruct((t, B_CHANNELS), F32),
        jax.ShapeDtypeStruct((t, C_QK), BF16),
        jax.ShapeDtypeStruct((BATCH, p, C_QK), BF16),
        jax.ShapeDtypeStruct((BATCH, C_HEADS, nblk, C_V_DIM + ONES_ROWS, tm), BF16),
        jax.ShapeDtypeStruct((t, N_BRANCH * D_MODEL), BF16),
    )
    out_specs = (
        pl.BlockSpec((tm, A_Q), row),
        pl.BlockSpec((1, tm, A_KV), key_map),
        pl.BlockSpec((1, A_KV_HEADS, 1, HEAD_DIM + ONES_ROWS, tm), vt_map),
        pl.BlockSpec((tm, B_CHANNELS), row),
        pl.BlockSpec((tm, C_QK), row),
        pl.BlockSpec((1, tm, C_QK), key_map),
        pl.BlockSpec((1, C_HEADS, 1, C_V_DIM + ONES_ROWS, tm), vt_map),
        pl.BlockSpec((tm, N_BRANCH * D_MODEL), row),
    )
    in_specs = [
        pl.BlockSpec((tm, D_MODEL), row),
        pl.BlockSpec((1, 1, 6 * D_MODEL), lambda i: (mod_row(i), 0, 0)),
        _resident((1, D_MODEL)),
        _resident((D_MODEL, IN_WIDTH)),
        _resident((1, N_BRANCH * D_MODEL)),
        _resident((1, A_Q)),
        _resident((1, A_KV)),
        _resident((A_Q, A_Q)),
        pl.BlockSpec((tm, LANES), lambda i: (i % nblk, 0)),
        pl.BlockSpec((tm, LANES), lambda i: (i % nblk, 0)),
    ]
    return pl.pallas_call(
        functools.partial(_inproj_kernel, rope=rope, tm=tm),
        grid=(t // tm,),
        in_specs=in_specs,
        out_specs=out_specs,
        out_shape=out_shape,
        compiler_params=_cparams("parallel"),
        name="inproj_rope" if rope else "inproj_ctx",
    )(x, mod3, g1, w_in, b_gate, qn, kn, bd, cos_t, sin_t)


N_PROBLEMS = 8
STAGE_DELAYS = (3, 4, 5)
ATT_CHUNK = 256


def _half_queries(q_ref, group, rows):
    qg = q_ref[:, group * LANES:(group + 1) * LANES]
    lane = lax.broadcasted_iota(jnp.int32, (rows, LANES), 1)
    zero = jnp.zeros_like(qg)
    return jnp.where(lane < HEAD_DIM, qg, zero), jnp.where(lane >= HEAD_DIM, qg, zero)


def _attend_all(q_ref, k_lat_ref, k_ctx_ref, vt_lat_ref, vt_ctx_ref, qx_ref, m_ref, acc_ref, k_lane0, vgroup, tq):
    for g in range(N_PROBLEMS // 2):
        qx_ref[2 * g], qx_ref[2 * g + 1] = _half_queries(q_ref, g, tq)
    m_ref[...] = jnp.full(m_ref.shape, -jnp.inf, F32)
    acc_ref[...] = jnp.zeros(acc_ref.shape, F32)
    dn = (((1,), (1,)), ((), ()))

    n_lat = k_lat_ref.shape[1]
    chunks = [(k_lat_ref, vt_lat_ref, r0, ATT_CHUNK) for r0 in range(0, n_lat, ATT_CHUNK)]
    chunks.append((k_ctx_ref, vt_ctx_ref, 0, k_ctx_ref.shape[1]))
    items = [chunk + (i,) for chunk in chunks for i in range(N_PROBLEMS)]

    def scores(item):
        k_ref, _, r0, rows, i = item
        k = k_ref[0, r0:r0 + rows, k_lane0(i):k_lane0(i) + LANES]
        return lax.dot_general(k, qx_ref[i], dn, preferred_element_type=F32)

    def weighted_values(item, p):
        _, vt_ref, r0, rows, i = item
        blk = vt_ref.shape[4]
        width = min(blk, rows)
        out = None
        for b in range(0, rows, width):
            bi, off = divmod(r0 + b, blk)
            part = _dot(vt_ref[0, vgroup(i), bi, :, off:off + width], p[b:b + width])
            out = part if out is None else out + part
        return out

    d_max, d_exp, d_pv = STAGE_DELAYS
    st, m_pair, p = {}, {}, {}
    for t in range(len(items) + d_pv):
        if t < len(items):
            st[t] = scores(items[t])
        n = t - d_max
        if 0 <= n < len(items):
            i = items[n][4]
            m_old = m_ref[i]
            m_pair[n] = (m_old, jnp.maximum(m_old, jnp.max(st[n], axis=0, keepdims=True)))
        n = t - d_exp
        if 0 <= n < len(items):
            p[n] = jnp.exp2((st.pop(n) - m_pair[n][1]).astype(BF16))
        n = t - d_pv
        if 0 <= n < len(items):
            i = items[n][4]
            m_old, m_new = m_pair.pop(n)
            acc_ref[i] = jnp.exp2(m_old - m_new) * acc_ref[i] + weighted_values(items[n], p.pop(n))
            m_ref[i] = m_new


def _gqa_kernel(q_ref, k_lat_ref, k_ctx_ref, vt_lat_ref, vt_ctx_ref, o_ref, qx_ref, m_ref, acc_ref, *, tq):
    _attend_all(q_ref, k_lat_ref, k_ctx_ref, vt_lat_ref, vt_ctx_ref, qx_ref, m_ref, acc_ref,
                lambda i: 0, lambda i: i % 2, tq)
    for pair in range(A_GROUP):
        outs = [acc_ref[2 * pair + h, 0:HEAD_DIM] / acc_ref[2 * pair + h, HEAD_DIM:HEAD_DIM + 1] for h in range(2)]
        o_ref[:, pair * LANES:(pair + 1) * LANES] = jnp.concatenate(outs, axis=0).T.astype(BF16)


def _diff_lambda(lq1_ref, lk1_ref, lq2_ref, lk2_ref, lam_init):
    return (jnp.exp(jnp.sum(lq1_ref[...] * lk1_ref[...], axis=-1, keepdims=True))
            - jnp.exp(jnp.sum(lq2_ref[...] * lk2_ref[...], axis=-1, keepdims=True)) + lam_init)


def _diff_combine(o1, o2, lam, subg, lam_init):
    o = o1 - lam * o2
    o = o * lax.rsqrt(jnp.mean(o * o, axis=0, keepdims=True) + EPS)
    return o.T * subg * (1.0 - lam_init)


def _diff_kernel(q_ref, k_lat_ref, k_ctx_ref, vt_lat_ref, vt_ctx_ref, lq1_ref, lk1_ref, lq2_ref, lk2_ref,
                 subg_ref, o_ref, qx_ref, m_ref, acc_ref, *, tq, lam_init):
    _attend_all(q_ref, k_lat_ref, k_ctx_ref, vt_lat_ref, vt_ctx_ref, qx_ref, m_ref, acc_ref,
                lambda i: (i // 2) * LANES, lambda i: i // 2, tq)
    lam = _diff_lambda(lq1_ref, lk1_ref, lq2_ref, lk2_ref, lam_init)
    for head in range(C_HEADS):
        outs = [acc_ref[2 * head + h, 0:C_V_DIM] / acc_ref[2 * head + h, C_V_DIM:C_V_DIM + 1] for h in range(2)]
        o_ref[:, head * LANES:(head + 1) * LANES] = _diff_combine(
            outs[0], outs[1], lam, subg_ref[...], lam_init).astype(BF16)


def _attention_call(kernel, name, q, k_lat, k_ctx, vt_lat, vt_ctx, extra, *, tokens_per_batch, tq):
    nq = tokens_per_batch // tq
    kw = k_lat.shape[-1]
    g, dvx = vt_lat.shape[1], vt_lat.shape[3]
    bmap3 = lambda b, i: (b, 0, 0)
    bmap5 = lambda b, i: (b, 0, 0, 0, 0)
    in_specs = [
        pl.BlockSpec((tq, q.shape[1]), lambda b, i: (b * nq + i, 0)),
        pl.BlockSpec((1, k_lat.shape[1], kw), bmap3),
        pl.BlockSpec((1, k_ctx.shape[1], kw), bmap3),
        pl.BlockSpec((1, g, vt_lat.shape[2], dvx, vt_lat.shape[4]), bmap5),
        pl.BlockSpec((1, g, 1, dvx, vt_ctx.shape[4]), bmap5),
    ] + [_resident(e.shape) for e in extra]
    return pl.pallas_call(
        kernel,
        grid=(BATCH, nq),
        in_specs=in_specs,
        out_specs=pl.BlockSpec((tq, q.shape[1]), lambda b, i: (b * nq + i, 0)),
        out_shape=jax.ShapeDtypeStruct(q.shape, BF16),
        scratch_shapes=[pltpu.VMEM((N_PROBLEMS, tq, LANES), BF16), pltpu.VMEM((N_PROBLEMS, 1, tq), F32),
                        pltpu.VMEM((N_PROBLEMS, dvx, tq), F32)],
        compiler_params=_cparams("parallel", "arbitrary"),
        name=name,
    )(q, k_lat, k_ctx, vt_lat, vt_ctx, *extra)


def _attend_ctx(q_ext, k, vt):
    dn = (((1,), (1,)), ((), ()))
    st = lax.dot_general(k, q_ext, dn, preferred_element_type=F32)
    p = jnp.exp2(st - jnp.max(st, axis=0, keepdims=True)).astype(BF16)
    return _dot(vt, p)


def _gqa_ctx_kernel(q_ref, k_ref, vt_ref, o_ref, *, tq):
    for pair in range(A_GROUP):
        outs = []
        for half, q_ext in enumerate(_half_queries(q_ref, pair, tq)):
            acc = _attend_ctx(q_ext, k_ref[0], vt_ref[0, half, 0])
            outs.append(acc[0:HEAD_DIM] / acc[HEAD_DIM:HEAD_DIM + 1])
        o_ref[:, pair * LANES:(pair + 1) * LANES] = jnp.concatenate(outs, axis=0).T.astype(BF16)


def _diff_ctx_kernel(q_ref, k_ref, vt_ref, lq1_ref, lk1_ref, lq2_ref, lk2_ref, subg_ref, o_ref, *, tq, lam_init):
    lam = _diff_lambda(lq1_ref, lk1_ref, lq2_ref, lk2_ref, lam_init)
    for head in range(C_HEADS):
        outs = []
        for q_ext in _half_queries(q_ref, head, tq):
            acc = _attend_ctx(q_ext, k_ref[0, :, head * LANES:(head + 1) * LANES], vt_ref[0, head, 0])
            outs.append(acc[0:C_V_DIM] / acc[C_V_DIM:C_V_DIM + 1])
        o_ref[:, head * LANES:(head + 1) * LANES] = _diff_combine(
            outs[0], outs[1], lam, subg_ref[...], lam_init).astype(BF16)


def _ctx_attention_call(kernel, name, q, k, vt, extra):
    tq = CTX_LEN
    g, dvx = vt.shape[1], vt.shape[3]
    in_specs = [
        pl.BlockSpec((tq, q.shape[1]), lambda b: (b, 0)),
        pl.BlockSpec((1, tq, k.shape[-1]), lambda b: (b, 0, 0)),
        pl.BlockSpec((1, g, 1, dvx, tq), lambda b: (b, 0, 0, 0, 0)),
    ] + [_resident(e.shape) for e in extra]
    return pl.pallas_call(
        kernel,
        grid=(BATCH,),
        in_specs=in_specs,
        out_specs=pl.BlockSpec((tq, q.shape[1]), lambda b: (b, 0)),
        out_shape=jax.ShapeDtypeStruct(q.shape, BF16),
        compiler_params=_cparams("parallel"),
        name=name,
    )(q, k, vt, *extra)


CONV_HALO = 16


SUBLANES = 8
CONV_ROWS = 32


def _conv_kernel(prev_ref, cur_ref, next_ref, w_ref, b_ref, g_ref, beta_ref, o_ref, win_ref, sh_ref, *, tc, nblk):
    j = pl.program_id(1)
    win_ref[0:CONV_HALO, :] = jnp.where(j > 0, prev_ref[...], 0.0)
    win_ref[CONV_HALO:CONV_HALO + tc, :] = cur_ref[...]
    win_ref[CONV_HALO + tc:CONV_HALO + tc + CONV_HALO, :] = jnp.where(j < nblk - 1, next_ref[...], 0.0)
    span = sh_ref.shape[1]
    for s in range(1, SUBLANES):
        sh_ref[s - 1] = win_ref[s:s + span, :]
    first = CONV_HALO - B_CONV_W // 2
    nsub = CONV_ROWS // SUBLANES
    for r in range(0, tc, CONV_ROWS):
        accs = [jnp.zeros((SUBLANES, B_CHANNELS), F32) + b_ref[...]] * nsub
        for k in range(B_CONV_W):
            a, s = divmod(first + k, SUBLANES)
            wk = w_ref[k]
            for q in range(nsub):
                lo = r + (a + q) * SUBLANES
                src = win_ref[lo:lo + SUBLANES, :] if s == 0 else sh_ref[s - 1, lo:lo + SUBLANES, :]
                accs[q] = accs[q] + src * wk
        acc = jnp.concatenate(accs, axis=0)
        mu = jnp.mean(acc, axis=-1, keepdims=True)
        cen = acc - mu
        var = jnp.mean(cen * cen, axis=-1, keepdims=True)
        y = cen * lax.rsqrt(var + EPS) * g_ref[...] + beta_ref[...]
        o_ref[r:r + CONV_ROWS, :] = _silu(y).astype(BF16)


def _conv_call(yb, dw_w, dw_b, ln_g, ln_b, *, tokens_per_batch):
    p = tokens_per_batch
    tc = 256
    nblk = p // tc
    r = tc // CONV_HALO
    nh = p // CONV_HALO
    return pl.pallas_call(
        functools.partial(_conv_kernel, tc=tc, nblk=nblk),
        grid=(BATCH, nblk),
        in_specs=[
            pl.BlockSpec((CONV_HALO, B_CHANNELS), lambda b, j: (b * nh + jnp.maximum(j * r - 1, 0), 0)),
            pl.BlockSpec((tc, B_CHANNELS), lambda b, j: (b * nblk + j, 0)),
            pl.BlockSpec((CONV_HALO, B_CHANNELS), lambda b, j: (b * nh + jnp.minimum((j + 1) * r, nh - 1), 0)),
            _resident(dw_w.shape),
            _resident((1, B_CHANNELS)),
            _resident((1, B_CHANNELS)),
            _resident((1, B_CHANNELS)),
        ],
        out_specs=pl.BlockSpec((tc, B_CHANNELS), lambda b, j: (b * nblk + j, 0)),
        out_shape=jax.ShapeDtypeStruct((BATCH * p, B_CHANNELS), BF16),
        scratch_shapes=[pltpu.VMEM((tc + 2 * CONV_HALO, B_CHANNELS), F32),
                        pltpu.VMEM((SUBLANES - 1, tc + 2 * CONV_HALO - SUBLANES, B_CHANNELS), F32)],
        compiler_params=_cparams("parallel", "arbitrary"),
        name="conformer_conv",
    )(yb, yb, yb, dw_w, dw_b, ln_g, ln_b)


def _merge_kernel(x_ref, mod_ref, oa_ref, ob_ref, oc_ref, gt_ref, wpa_ref, wpb_ref, wpc_ref, wout_ref, o_ref):
    d = D_MODEL
    m = (gt_ref[:, 0:d].astype(F32) * _dot(oa_ref[...], wpa_ref[...])
         + gt_ref[:, d:2 * d].astype(F32) * _dot(ob_ref[...], wpb_ref[...])
         + gt_ref[:, 2 * d:3 * d].astype(F32) * _dot(oc_ref[...], wpc_ref[...]))
    y = _dot(m.astype(BF16), wout_ref[...])
    gate1 = mod_ref[0][:, 2 * d:3 * d]
    o_ref[...] = x_ref[...] + gate1 * y


def _merge_call(x, mod3, mod_row, oa, ob, oc, gt, wpa, wpb, wpc, wout, *, tm):
    t = x.shape[0]
    row = lambda i: (i, 0)
    return pl.pallas_call(
        _merge_kernel,
        grid=(t // tm,),
        in_specs=[
            pl.BlockSpec((tm, D_MODEL), row),
            pl.BlockSpec((1, 1, 6 * D_MODEL), lambda i: (mod_row(i), 0, 0)),
            pl.BlockSpec((tm, A_Q), row),
            pl.BlockSpec((tm, B_CHANNELS), row),
            pl.BlockSpec((tm, C_V), row),
            pl.BlockSpec((tm, N_BRANCH * D_MODEL), row),
            _resident(wpa.shape), _resident(wpb.shape), _resident(wpc.shape), _resident(wout.shape),
        ],
        out_specs=pl.BlockSpec((tm, D_MODEL), row),
        out_shape=jax.ShapeDtypeStruct(x.shape, F32),
        compiler_params=_cparams("parallel"),
        name="branch_merge",
    )(x, mod3, oa, ob, oc, gt, wpa, wpb, wpc, wout)


FFN_SPLIT = 2


def _ffn_kernel(x_ref, mod_ref, g2_ref, w1_ref, w3_ref, w2_ref, o_ref):
    d = D_MODEL
    mod = mod_ref[0]
    x = x_ref[...]
    h = _norm_mod(x, g2_ref[...], mod[:, 3 * d:4 * d], mod[:, 4 * d:5 * d]).astype(BF16)
    fc = D_FF // FFN_SPLIT
    y = jnp.zeros(x.shape, F32)
    for s in range(FFN_SPLIT):
        a = _dot(h, w1_ref[:, s * fc:(s + 1) * fc])
        b = _dot(h, w3_ref[:, s * fc:(s + 1) * fc])
        y = y + _dot((_silu(a) * b).astype(BF16), w2_ref[s * fc:(s + 1) * fc, :])
    o_ref[...] = x + mod[:, 5 * d:6 * d] * y


def _ffn_call(x, mod3, mod_row, g2, w1, w3, w2, *, tm):
    t = x.shape[0]
    row = lambda i: (i, 0)
    return pl.pallas_call(
        _ffn_kernel,
        grid=(t // tm,),
        in_specs=[
            pl.BlockSpec((tm, D_MODEL), row),
            pl.BlockSpec((1, 1, 6 * D_MODEL), lambda i: (mod_row(i), 0, 0)),
            _resident((1, D_MODEL)),
            _resident(w1.shape), _resident(w3.shape), _resident(w2.shape),
        ],
        out_specs=pl.BlockSpec((tm, D_MODEL), row),
        out_shape=jax.ShapeDtypeStruct(x.shape, F32),
        compiler_params=_cparams("parallel"),
        name="dense_swiglu",
    )(x, mod3, g2, w1, w3, w2)


MOE_F_TILE = 1792
MOE_TM = 512
TOP_K = 2


def _router_kernel(x_ref, mod_ref, g2_ref, wr_ref, br_ref, h_ref, sel_ref, wts_ref):
    d = D_MODEL
    mod = mod_ref[0]
    h = _norm_mod(x_ref[...], g2_ref[...], mod[:, 3 * d:4 * d], mod[:, 4 * d:5 * d])
    h_ref[...] = h
    logits = jnp.dot(h, wr_ref[...], preferred_element_type=F32, precision=lax.Precision.HIGHEST) + br_ref[...]
    lane = lax.broadcasted_iota(jnp.int32, logits.shape, 1)
    v1 = jnp.max(logits, axis=-1, keepdims=True)
    i1 = jnp.min(jnp.where(logits == v1, lane, LANES), axis=-1, keepdims=True)
    rest = jnp.where(lane == i1, -jnp.inf, logits)
    v2 = jnp.max(rest, axis=-1, keepdims=True)
    i2 = jnp.min(jnp.where(rest == v2, lane, LANES), axis=-1, keepdims=True)
    e2 = jnp.exp(v2 - v1)
    w1 = 1.0 / (1.0 + e2)
    sel_ref[...] = jnp.where(lane == 0, i1, jnp.where(lane == 1, i2, 0))
    wts_ref[...] = jnp.where(lane == 0, w1, jnp.where(lane == 1, e2 * w1, 0.0))


def _router_call(x, mod3, mod_row, g2, wr, br, *, tm):
    t = x.shape[0]
    row = lambda i: (i, 0)
    return pl.pallas_call(
        _router_kernel,
        grid=(t // tm,),
        in_specs=[
            pl.BlockSpec((tm, D_MODEL), row),
            pl.BlockSpec((1, 1, 6 * D_MODEL), lambda i: (mod_row(i), 0, 0)),
            _resident((1, D_MODEL)), _resident((D_MODEL, LANES)), _resident((1, LANES)),
        ],
        out_specs=(pl.BlockSpec((tm, D_MODEL), row), pl.BlockSpec((tm, LANES), row), pl.BlockSpec((tm, LANES), row)),
        out_shape=(jax.ShapeDtypeStruct((t, D_MODEL), F32), jax.ShapeDtypeStruct((t, LANES), jnp.int32),
                   jax.ShapeDtypeStruct((t, LANES), F32)),
        compiler_params=_cparams("parallel"),
        name="moe_router",
    )(x, mod3, g2, wr, br)


def _dispatch_plan(sel, n_rows):
    e_flat = sel.reshape(-1)
    n_assign = e_flat.shape[0]
    onehot = (e_flat[:, None] == jnp.arange(N_EXPERTS, dtype=jnp.int32)[None, :]).astype(jnp.int32)
    csum = jnp.cumsum(onehot, axis=0)
    rank = jnp.take_along_axis(csum, e_flat[:, None], axis=1)[:, 0] - 1
    padded = (csum[-1] + MOE_TM - 1) // MOE_TM * MOE_TM
    end = jnp.cumsum(padded)
    pos = (end - padded)[e_flat] + rank
    row_token = jnp.zeros((n_rows,), jnp.int32).at[pos].set(
        jnp.arange(n_assign, dtype=jnp.int32) // TOP_K, unique_indices=True, mode="promise_in_bounds")
    tile_start = jnp.arange(n_rows // MOE_TM, dtype=jnp.int32) * MOE_TM
    tile_expert = jnp.minimum(jnp.sum(tile_start[:, None] >= end[None, :], axis=1), N_EXPERTS - 1).astype(jnp.int32)
    n_tiles = (end[-1:] // MOE_TM).astype(jnp.int32)
    return row_token, pos.astype(jnp.int32), tile_expert, n_tiles


def _gather_rows(idx_ref, idx0, src_hbm, dst_ref, sem, row0, n, inline=False):
    def body(r, carry):
        src = src_hbm.at[pl.ds(idx_ref[idx0 + r], 1)]
        pltpu.make_async_copy(src, dst_ref.at[pl.ds(row0 + r, 1)], sem).start()
        return carry
    if inline:
        for r in range(n):
            body(r, 0)
    else:
        lax.fori_loop(0, n, body, 0, unroll=8)


def _wait_rows(src_hbm, dst_ref, sem):
    pltpu.make_async_copy(src_hbm.at[pl.ds(0, dst_ref.shape[0])], dst_ref, sem).wait()


def _expert_kernel(row_token_ref, tile_expert_ref, n_tiles_ref, h_hbm, w1_ref, w3_ref, w2_ref, y_ref,
                   xbuf_ref, acc_ref, sem_ref):
    j = pl.program_id(0)
    f = pl.program_id(1)
    nf = pl.num_programs(1)
    slot = j % 2
    valid = j < n_tiles_ref[0]

    @pl.when(jnp.logical_and(j == 0, f == 0))
    def _():
        _gather_rows(row_token_ref, 0, h_hbm, xbuf_ref.at[0], sem_ref.at[0], 0, MOE_TM)

    @pl.when(jnp.logical_and(valid, f == 0))
    def _():
        _wait_rows(h_hbm, xbuf_ref.at[slot], sem_ref.at[slot])

    def compute(gather_next):
        if gather_next:
            share = MOE_TM // (D_FF_EXPERT // MOE_F_TILE)
            _gather_rows(row_token_ref, (j + 1) * MOE_TM + f * share, h_hbm, xbuf_ref.at[1 - slot],
                         sem_ref.at[1 - slot], f * share, share, inline=True)
        x = xbuf_ref[slot].astype(BF16)
        z = (_silu(_dot(x, w1_ref[0])) * _dot(x, w3_ref[0])).astype(BF16)
        part = _dot(z, w2_ref[0])

        @pl.when(f == 0)
        def _():
            acc_ref[...] = part

        @pl.when(f > 0)
        def _():
            acc_ref[...] += part

        @pl.when(f == nf - 1)
        def _():
            y_ref[...] = acc_ref[...]

    has_next = j + 1 < n_tiles_ref[0]
    pl.when(has_next)(functools.partial(compute, True))
    pl.when(jnp.logical_and(valid, jnp.logical_not(has_next)))(functools.partial(compute, False))

    @pl.when(jnp.logical_and(jnp.logical_not(valid), f == nf - 1))
    def _():
        y_ref[...] = jnp.zeros(y_ref.shape, F32)


def _expert_call(h, row_token, tile_expert, n_tiles, w1, w3, w2):
    n_rows = row_token.shape[0]
    nt = n_rows // MOE_TM
    nf = D_FF_EXPERT // MOE_F_TILE

    def f_eff(j, f, n):
        return jnp.where(j < n[0], f, nf - 1)

    grid_spec = pltpu.PrefetchScalarGridSpec(
        num_scalar_prefetch=3,
        grid=(nt, nf),
        in_specs=[
            pl.BlockSpec(memory_space=pl.ANY),
            pl.BlockSpec((1, D_MODEL, MOE_F_TILE), lambda j, f, rt, te, n: (te[j], 0, f_eff(j, f, n))),
            pl.BlockSpec((1, D_MODEL, MOE_F_TILE), lambda j, f, rt, te, n: (te[j], 0, f_eff(j, f, n))),
            pl.BlockSpec((1, MOE_F_TILE, D_MODEL), lambda j, f, rt, te, n: (te[j], f_eff(j, f, n), 0)),
        ],
        out_specs=pl.BlockSpec((MOE_TM, D_MODEL), lambda j, f, rt, te, n: (j, 0)),
        scratch_shapes=[pltpu.VMEM((2, MOE_TM, D_MODEL), F32), pltpu.VMEM((MOE_TM, D_MODEL), F32),
                        pltpu.SemaphoreType.DMA((2,))],
    )
    return pl.pallas_call(
        _expert_kernel,
        grid_spec=grid_spec,
        out_shape=jax.ShapeDtypeStruct((n_rows, D_MODEL), F32),
        compiler_params=_cparams("arbitrary", "arbitrary"),
        name="moe_experts",
    )(row_token, tile_expert, n_tiles, h, w1, w3, w2)


def _combine_kernel(pos_ref, y_hbm, x_ref, mod_ref, wts_ref, fg_ref, o_ref, ybuf_ref, sem_ref, *, tc, final_norm):
    i = pl.program_id(0)
    slot = i % 2
    rows = TOP_K * tc

    @pl.when(i == 0)
    def _():
        _gather_rows(pos_ref, 0, y_hbm, ybuf_ref.at[0], sem_ref.at[0], 0, rows)

    @pl.when(i + 1 < pl.num_programs(0))
    def _():
        _gather_rows(pos_ref, (i + 1) * rows, y_hbm, ybuf_ref.at[1 - slot], sem_ref.at[1 - slot], 0, rows,
                     inline=True)

    _wait_rows(y_hbm, ybuf_ref.at[slot], sem_ref.at[slot])
    wts = wts_ref[...]
    moe = wts[:, 0:1] * ybuf_ref[slot, 0:tc, :] + wts[:, 1:2] * ybuf_ref[slot, tc:rows, :]
    y = x_ref[...] + mod_ref[0][:, 5 * D_MODEL:6 * D_MODEL] * moe
    if final_norm:
        y = y * lax.rsqrt(jnp.mean(y * y, axis=-1, keepdims=True) + EPS) * fg_ref[...]
    o_ref[...] = y


def _combine_call(x, mod3, mod_row, y, pos, wts, fg, *, tc, final_norm):
    t = x.shape[0]
    row = lambda i, p: (i, 0)
    pos_tiles = pos.reshape(t // tc, tc, TOP_K).transpose(0, 2, 1).reshape(-1)
    grid_spec = pltpu.PrefetchScalarGridSpec(
        num_scalar_prefetch=1,
        grid=(t // tc,),
        in_specs=[
            pl.BlockSpec(memory_space=pl.ANY),
            pl.BlockSpec((tc, D_MODEL), row),
            pl.BlockSpec((1, 1, 6 * D_MODEL), lambda i, p: (mod_row(i), 0, 0)),
            pl.BlockSpec((tc, LANES), row),
            pl.BlockSpec((1, D_MODEL), lambda i, p: (0, 0)),
        ],
        out_specs=pl.BlockSpec((tc, D_MODEL), row),
        scratch_shapes=[pltpu.VMEM((2, TOP_K * tc, D_MODEL), F32), pltpu.SemaphoreType.DMA((2,))],
    )
    return pl.pallas_call(
        functools.partial(_combine_kernel, tc=tc, final_norm=final_norm),
        grid_spec=grid_spec,
        out_shape=jax.ShapeDtypeStruct(x.shape, F32),
        compiler_params=_cparams("arbitrary"),
        name="moe_combine",
    )(pos_tiles, y, x, mod3, wts, fg)


def _moe_call(x, mod3, mod_row, g2, wr, br, w1, w3, w2, fg, *, tm, final_norm):
    t = x.shape[0]
    h, sel, wts = _router_call(x, mod3, mod_row, g2, wr, br, tm=tm)
    n_rows = TOP_K * t + N_EXPERTS * MOE_TM
    row_token, pos, tile_expert, n_tiles = _dispatch_plan(sel[:, :TOP_K], n_rows)
    y = _expert_call(h, row_token, tile_expert, n_tiles, w1, w3, w2)
    tc = 256
    return _combine_call(x, mod3, lambda i: mod_row(i * tc // tm), y, pos, wts, fg, tc=tc, final_norm=final_norm)


def _final_norm_kernel(x_ref, g_ref, o_ref):
    x = x_ref[...]
    o_ref[...] = x * lax.rsqrt(jnp.mean(x * x, axis=-1, keepdims=True) + EPS) * g_ref[...]


def _final_norm_call(x, g, *, tm):
    row = lambda i: (i, 0)
    return pl.pallas_call(
        _final_norm_kernel,
        grid=(x.shape[0] // tm,),
        in_specs=[pl.BlockSpec((tm, D_MODEL), row), _resident((1, D_MODEL))],
        out_specs=pl.BlockSpec((tm, D_MODEL), row),
        out_shape=jax.ShapeDtypeStruct(x.shape, F32),
        compiler_params=_cparams("parallel"),
        name="final_rmsnorm",
    )(x, g)


def _rope_tables():
    rows = SEQ // GRID_W
    row = jnp.repeat(jnp.arange(rows), GRID_W)
    col = jnp.tile(jnp.arange(GRID_W), rows)
    n_freq = HEAD_DIM // 4
    inv = ROPE_THETA ** (-jnp.arange(n_freq, dtype=F32) / n_freq)
    ang = jnp.stack([row, col], axis=-1).astype(F32)[:, :, None] * inv
    cos, sin = jnp.cos(ang), jnp.sin(ang)
    cos64 = jnp.concatenate([cos[:, 0], cos[:, 0], cos[:, 1], cos[:, 1]], axis=-1)
    sin64 = jnp.concatenate([-sin[:, 0], sin[:, 0], -sin[:, 1], sin[:, 1]], axis=-1)
    return jnp.tile(cos64, (1, 2)), jnp.tile(sin64, (1, 2))


def _pair_heads(w, axis):
    shp = w.shape
    w = w.reshape(shp[:axis] + (A_KV_HEADS, A_GROUP, HEAD_DIM) + shp[axis + 1:])
    w = jnp.swapaxes(w, axis, axis + 1)
    return w.reshape(shp)


def kernel(x, c, ctx, c_ctx, w_mod, b_mod, norm1_g, norm2_g, w_in, b_gate, a_qn_g, a_kn_g, b_dw_w, b_dw_b,
           b_ln_g, b_ln_b, c_lq1, c_lk1, c_lq2, c_lk2, c_subln_g, w_pa, w_pb, w_pc, w_out, ffn_w1, ffn_w3,
           ffn_w2, moe_router, moe_router_b, moe_w1, moe_w3, moe_w2, final_g):
    d = D_MODEL
    t_lat = BATCH * SEQ
    xl = x.reshape(t_lat, d)
    xc = ctx.reshape(BATCH * CTX_LEN, d)

    cc = jnp.concatenate([c, c_ctx[None, :], jnp.zeros((3, d), F32)], axis=0)
    mod_all = _mod_call(cc, w_mod, b_mod)

    cos_t, sin_t = _rope_tables()
    head_id = np.arange(A_Q) // HEAD_DIM
    bd = jnp.asarray(head_id[:, None] == head_id[None, :], BF16)

    lat_tiles_per_batch = SEQ // KEY_CHUNK
    lat_row = lambda i: i // lat_tiles_per_batch
    ctx_row = lambda i: 4
    tm = KEY_CHUNK

    for i in range(DEPTH):
        last = i == DEPTH - 1
        lam_init = 0.8 - 0.6 * math.exp(-0.3 * i)
        mod3 = mod_all[i].reshape(8, 1, 6 * d)
        g1 = norm1_g[i].reshape(1, d)
        g2 = norm2_g[i].reshape(1, d)
        w_in_i = jnp.concatenate([_pair_heads(w_in[i][:, :A_Q], 1), w_in[i][:, A_Q:]], axis=1).astype(BF16)
        bg = b_gate[i].reshape(1, -1)
        qn = jnp.tile(a_qn_g[i], A_HEADS).reshape(1, A_Q)
        kn = jnp.tile(a_kn_g[i], A_KV_HEADS).reshape(1, A_KV)
        lam_vecs = [v[i].reshape(1, HEAD_DIM) for v in (c_lq1, c_lk1, c_lq2, c_lk2)]
        subg = c_subln_g[i].reshape(1, C_V_DIM)
        wpa = _pair_heads(w_pa[i], 0).astype(BF16)
        wpb, wpc, wout = w_pb[i].astype(BF16), w_pc[i].astype(BF16), w_out[i].astype(BF16)
        dw_w = jnp.broadcast_to(b_dw_w[i][:, None, :], (B_CONV_W, SUBLANES, B_CHANNELS))
        conv_p = (dw_w, b_dw_b[i].reshape(1, -1), b_ln_g[i].reshape(1, -1), b_ln_b[i].reshape(1, -1))

        proj = functools.partial(_inproj_call, g1=g1, w_in=w_in_i, b_gate=bg, qn=qn, kn=kn, bd=bd,
                                 cos_t=cos_t, sin_t=sin_t)
        qa, ka, vta, yb, qc, kc, vtc, gt = proj(xl, mod3, lat_row, tokens_per_batch=SEQ, rope=True)
        qa_c, ka_c, vta_c, yb_c, qc_c, kc_c, vtc_c, gt_c = proj(xc, mod3, ctx_row, tokens_per_batch=CTX_LEN,
                                                                rope=False)

        oa = _attention_call(functools.partial(_gqa_kernel, tq=256), "gqa_attention", qa, ka, ka_c, vta, vta_c,
                             [], tokens_per_batch=SEQ, tq=256)
        oc = _attention_call(functools.partial(_diff_kernel, tq=256, lam_init=lam_init), "diff_attention",
                             qc, kc, kc_c, vtc, vtc_c, lam_vecs + [subg], tokens_per_batch=SEQ, tq=256)
        ob = _conv_call(yb, *conv_p, tokens_per_batch=SEQ)
        xl = _merge_call(xl, mod3, lat_row, oa, ob, oc, gt, wpa, wpb, wpc, wout, tm=tm)

        if not last:
            oa_c = _ctx_attention_call(functools.partial(_gqa_ctx_kernel, tq=CTX_LEN), "gqa_attention_ctx",
                                       qa_c, ka_c, vta_c, [])
            oc_c = _ctx_attention_call(functools.partial(_diff_ctx_kernel, tq=CTX_LEN, lam_init=lam_init),
                                       "diff_attention_ctx", qc_c, kc_c, vtc_c, lam_vecs + [subg])
            ob_c = _conv_call(yb_c, *conv_p, tokens_per_batch=CTX_LEN)
            xc = _merge_call(xc, mod3, ctx_row, oa_c, ob_c, oc_c, gt_c, wpa, wpb, wpc, wout, tm=tm)

        j = i // 2
        if i % 2 == 0:
            ffn_w = (ffn_w1[j].astype(BF16), ffn_w3[j].astype(BF16), ffn_w2[j].astype(BF16))
            xl = _ffn_call(xl, mod3, lat_row, g2, *ffn_w, tm=tm)
            if last:
                xl = _final_norm_call(xl, final_g.reshape(1, d), tm=tm)
            else:
                xc = _ffn_call(xc, mod3, ctx_row, g2, *ffn_w, tm=tm)
        else:
            wr = jnp.zeros((d, LANES), F32).at[:, :N_EXPERTS].set(moe_router[j])
            br = jnp.full((1, LANES), -jnp.inf, F32).at[0, :N_EXPERTS].set(moe_router_b[j])
            moe_w = (moe_w1[j].astype(BF16), moe_w3[j].astype(BF16), moe_w2[j].astype(BF16))
            fg = final_g.reshape(1, d)
            xl = _moe_call(xl, mod3, lat_row, g2, wr, br, *moe_w, fg, tm=tm, final_norm=last)
            if not last:
                xc = _moe_call(xc, mod3, ctx_row, g2, wr, br, *moe_w, fg, tm=tm, final_norm=False)
    return xl.reshape(BATCH, SEQ, d)
```

```python
import functools
import math

import numpy as np
import jax
import jax.numpy as jnp
from jax import lax
from jax.experimental import pallas as pl
from jax.experimental.pallas import tpu as pltpu

F32 = jnp.float32
BF16 = jnp.bfloat16

D_MODEL = 1024
BATCH = 4
SEQ = 4096
DEPTH = 2
GRID_W = 64
CTX_LEN = 256
EPS = 1e-6
ROPE_THETA = 10000.0

HEAD_DIM = 64
A_HEADS = 8
A_KV_HEADS = 2
A_GROUP = 4
A_Q = 512
A_KV = 128
B_CHANNELS = 512
B_CONV_W = 31
C_HEADS = 4
C_V_DIM = 128
C_QK = 512
C_V = 512
N_BRANCH = 3
IN_SPLITS = [A_Q, A_KV, A_KV, 2 * B_CHANNELS, C_QK, C_QK, C_V, N_BRANCH * D_MODEL]
IN_WIDTH = sum(IN_SPLITS)
IN_OFF = [0] + [int(v) for v in np.cumsum(IN_SPLITS)]
D_FF = 2816
N_EXPERTS = 8
D_FF_EXPERT = 3584

Q_SCALE = HEAD_DIM ** -0.5 * 1.4426950408889634

LANES = 128
ONES_ROWS = 16
KEY_CHUNK = 512
VMEM_LIMIT = 56 * 1024 * 1024


def _cparams(*sem):
    return pltpu.CompilerParams(dimension_semantics=sem, vmem_limit_bytes=VMEM_LIMIT)


def _resident(shape):
    nd = len(shape)
    return pl.BlockSpec(shape, lambda *_: (0,) * nd, pipeline_mode=pl.Buffered(1))


def _sigmoid(v):
    return 1.0 / (1.0 + jnp.exp(-v))


def _silu(v):
    return v * _sigmoid(v)


def _norm_mod(x, g, shift, scale):
    ms = jnp.mean(x * x, axis=-1, keepdims=True)
    return (x * lax.rsqrt(ms + EPS) * g) * (1.0 + scale) + shift


def _dot(a, b):
    return jnp.dot(a, b, preferred_element_type=F32)


def _dot_split(a, b):
    ah, bh = a.astype(BF16), b.astype(BF16)
    al, bl = (a - ah.astype(F32)).astype(BF16), (b - bh.astype(F32)).astype(BF16)
    return _dot(ah, bh) + (_dot(ah, bl) + _dot(al, bh))


def _mod_kernel(c_ref, w_ref, b_ref, o_ref):
    s = _silu(c_ref[...])
    o_ref[0] = jnp.dot(s, w_ref[0], preferred_element_type=F32, precision=lax.Precision.HIGHEST) + b_ref[0]


def _mod_call(cc, w_mod, b_mod):
    n = 6 * D_MODEL
    tn = D_MODEL
    return pl.pallas_call(
        _mod_kernel,
        grid=(DEPTH, n // tn),
        in_specs=[
            pl.BlockSpec((8, D_MODEL), lambda i, j: (0, 0)),
            pl.BlockSpec((1, D_MODEL, tn), lambda i, j: (i, 0, j)),
            pl.BlockSpec((1, 1, tn), lambda i, j: (i, 0, j)),
        ],
        out_specs=pl.BlockSpec((1, 8, tn), lambda i, j: (i, 0, j)),
        out_shape=jax.ShapeDtypeStruct((DEPTH, 8, n), F32),
        compiler_params=_cparams("arbitrary", "arbitrary"),
        name="adaln_mod",
    )(cc, w_mod, b_mod.reshape(DEPTH, 1, n))


def _segment_mean_sq(u, bd):
    sq = u * u
    hi = sq.astype(BF16)
    lo = (sq - hi.astype(F32)).astype(BF16)
    return (_dot(hi, bd) + _dot(lo, bd)) * (1.0 / HEAD_DIM)


def _rope_group(xg, cos, sin, lo_mask):
    partner = jnp.where(lo_mask, pltpu.roll(xg, LANES - 16, 1), pltpu.roll(xg, 16, 1))
    return xg * cos + partner * sin


def _inproj_kernel(x_ref, mod_ref, g1_ref, w_ref, bgate_ref, qn_ref, kn_ref, bd_ref, cos_ref, sin_ref,
                   qa_ref, ka_ref, vta_ref, yb_ref, qc_ref, kc_ref, vtc_ref, gt_ref, *, rope, tm):
    mod = mod_ref[0]
    h = _norm_mod(x_ref[...], g1_ref[...], mod[:, 0:D_MODEL], mod[:, D_MODEL:2 * D_MODEL]).astype(BF16)

    def proj(c0, c1):
        return _dot(h, w_ref[:, c0:c1])

    if rope:
        cos = cos_ref[...]
        sin = sin_ref[...]
        lane = lax.broadcasted_iota(jnp.int32, (tm, LANES), 1)
        lo_mask = (lane % 32) < 16

    def rot(u, j):
        ug = u[:, j * LANES:(j + 1) * LANES]
        return _rope_group(ug, cos, sin, lo_mask) if rope else ug

    ones_pat = jnp.where(lax.broadcasted_iota(jnp.int32, (ONES_ROWS, tm), 0) == 0, 1.0, 0.0).astype(BF16)

    def finish_aq(u):
        u = u * lax.rsqrt(_segment_mean_sq(u, bd_ref[...]) + EPS) * (qn_ref[...] * Q_SCALE)
        for j in range(A_Q // LANES):
            qa_ref[:, j * LANES:(j + 1) * LANES] = rot(u, j).astype(BF16)

    def finish_ak(u):
        u = u * lax.rsqrt(_segment_mean_sq(u, bd_ref[0:A_KV, 0:A_KV]) + EPS) * kn_ref[...]
        ka_ref[0] = rot(u, 0).astype(BF16)

    def finish_av(u):
        ut = u.T
        for j in range(A_KV_HEADS):
            vta_ref[0, j, 0, 0:HEAD_DIM, :] = ut[j * HEAD_DIM:(j + 1) * HEAD_DIM].astype(BF16)
            vta_ref[0, j, 0, HEAD_DIM:HEAD_DIM + ONES_ROWS, :] = ones_pat

    def finish_b(u):
        yb_ref[...] = u[:, 0:B_CHANNELS] * _sigmoid(u[:, B_CHANNELS:2 * B_CHANNELS])

    def finish_cq(u):
        u = u * Q_SCALE
        for j in range(C_QK // LANES):
            qc_ref[:, j * LANES:(j + 1) * LANES] = rot(u, j).astype(BF16)

    def finish_ck(u):
        for j in range(C_QK // LANES):
            kc_ref[0, :, j * LANES:(j + 1) * LANES] = rot(u, j).astype(BF16)

    def finish_cv(u):
        ut = u.T
        for j in range(C_HEADS):
            vtc_ref[0, j, 0, 0:C_V_DIM, :] = ut[j * C_V_DIM:(j + 1) * C_V_DIM].astype(BF16)
            vtc_ref[0, j, 0, C_V_DIM:C_V_DIM + ONES_ROWS, :] = ones_pat

    def finish_gate(j, u):
        gt_ref[:, j * D_MODEL:(j + 1) * D_MODEL] = _sigmoid(
            u + bgate_ref[:, j * D_MODEL:(j + 1) * D_MODEL]).astype(BF16)

    stages = [((IN_OFF[g], IN_OFF[g + 1]), fin) for g, fin in enumerate(
        (finish_aq, finish_ak, finish_av, finish_b, finish_cq, finish_ck, finish_cv))]
    stages += [((IN_OFF[7] + j * D_MODEL, IN_OFF[7] + (j + 1) * D_MODEL), functools.partial(finish_gate, j))
               for j in range(N_BRANCH)]
    pending = proj(*stages[0][0])
    for n, (_, finish) in enumerate(stages):
        u = pending
        if n + 1 < len(stages):
            pending = proj(*stages[n + 1][0])
        finish(u)


def _inproj_call(x, mod3, mod_row, g1, w_in, b_gate, qn, kn, bd, cos_t, sin_t, *, tokens_per_batch, rope):
    p = tokens_per_batch
    tm = min(p, KEY_CHUNK)
    nblk = p // tm
    t = BATCH * p
    row = lambda i: (i, 0)
    vt_map = lambda i: (i // nblk, 0, i % nblk, 0, 0)
    key_map = lambda i: (i // nblk, i % nblk, 0)
    out_shape = (
        jax.ShapeDtypeStruct((t, A_Q), BF16),
        jax.ShapeDtypeStruct((BATCH, p, A_KV), BF16),
        jax.ShapeDtypeStruct((BATCH, A_KV_HEADS, nblk, HEAD_DIM + ONES_ROWS, tm), BF16),
        jax.ShapeDtypeStruct((t, B_CHANNELS), F32),
        jax.ShapeDtypeStruct((t, C_QK), BF16),
        jax.ShapeDtypeStruct((BATCH, p, C_QK), BF16),
        jax.ShapeDtypeStruct((BATCH, C_HEADS, nblk, C_V_DIM + ONES_ROWS, tm), BF16),
        jax.ShapeDtypeStruct((t, N_BRANCH * D_MODEL), BF16),
    )
    out_specs = (
        pl.BlockSpec((tm, A_Q), row),
        pl.BlockSpec((1, tm, A_KV), key_map),
        pl.BlockSpec((1, A_KV_HEADS, 1, HEAD_DIM + ONES_ROWS, tm), vt_map),
        pl.BlockSpec((tm, B_CHANNELS), row),
        pl.BlockSpec((tm, C_QK), row),
        pl.BlockSpec((1, tm, C_QK), key_map),
        pl.BlockSpec((1, C_HEADS, 1, C_V_DIM + ONES_ROWS, tm), vt_map),
        pl.BlockSpec((tm, N_BRANCH * D_MODEL), row),
    )
    in_specs = [
        pl.BlockSpec((tm, D_MODEL), row),
        pl.BlockSpec((1, 1, 6 * D_MODEL), lambda i: (mod_row(i), 0, 0)),
        _resident((1, D_MODEL)),
        _resident((D_MODEL, IN_WIDTH)),
        _resident((1, N_BRANCH * D_MODEL)),
        _resident((1, A_Q)),
        _resident((1, A_KV)),
        _resident((A_Q, A_Q)),
        pl.BlockSpec((tm, LANES), lambda i: (i % nblk, 0)),
        pl.BlockSpec((tm, LANES), lambda i: (i % nblk, 0)),
    ]
    return pl.pallas_call(
        functools.partial(_inproj_kernel, rope=rope, tm=tm),
        grid=(t // tm,),
        in_specs=in_specs,
        out_specs=out_specs,
        out_shape=out_shape,
        compiler_params=_cparams("parallel"),
        name="inproj_rope" if rope else "inproj_ctx",
    )(x, mod3, g1, w_in, b_gate, qn, kn, bd, cos_t, sin_t)


N_PROBLEMS = 8
STAGE_DELAYS = (3, 4, 5)
ATT_CHUNK = 256


def _half_queries(q_ref, group, rows):
    qg = q_ref[:, group * LANES:(group + 1) * LANES]
    lane = lax.broadcasted_iota(jnp.int32, (rows, LANES), 1)
    zero = jnp.zeros_like(qg)
    return jnp.where(lane < HEAD_DIM, qg, zero), jnp.where(lane >= HEAD_DIM, qg, zero)


def _attend_all(q_ref, k_lat_ref, k_ctx_ref, vt_lat_ref, vt_ctx_ref, qx_ref, m_ref, acc_ref, k_lane0, vgroup, tq):
    for g in range(N_PROBLEMS // 2):
        qx_ref[2 * g], qx_ref[2 * g + 1] = _half_queries(q_ref, g, tq)
    m_ref[...] = jnp.full(m_ref.shape, -jnp.inf, F32)
    acc_ref[...] = jnp.zeros(acc_ref.shape, F32)
    dn = (((1,), (1,)), ((), ()))

    n_lat = k_lat_ref.shape[1]
    chunks = [(k_lat_ref, vt_lat_ref, r0, ATT_CHUNK) for r0 in range(0, n_lat, ATT_CHUNK)]
    chunks.append((k_ctx_ref, vt_ctx_ref, 0, k_ctx_ref.shape[1]))
    items = [chunk + (i,) for chunk in chunks for i in range(N_PROBLEMS)]

    def scores(item):
        k_ref, _, r0, rows, i = item
        k = k_ref[0, r0:r0 + rows, k_lane0(i):k_lane0(i) + LANES]
        return lax.dot_general(k, qx_ref[i], dn, preferred_element_type=F32)

    def weighted_values(item, p):
        _, vt_ref, r0, rows, i = item
        blk = vt_ref.shape[4]
        width = min(blk, rows)
        out = None
        for b in range(0, rows, width):
            bi, off = divmod(r0 + b, blk)
            part = _dot(vt_ref[0, vgroup(i), bi, :, off:off + width], p[b:b + width])
            out = part if out is None else out + part
        return out

    d_max, d_exp, d_pv = STAGE_DELAYS
    st, m_pair, p = {}, {}, {}
    for t in range(len(items) + d_pv):
        if t < len(items):
            st[t] = scores(items[t])
        n = t - d_max
        if 0 <= n < len(items):
            i = items[n][4]
            m_old = m_ref[i]
            m_pair[n] = (m_old, jnp.maximum(m_old, jnp.max(st[n], axis=0, keepdims=True)))
        n = t - d_exp
        if 0 <= n < len(items):
            p[n] = jnp.exp2((st.pop(n) - m_pair[n][1]).astype(BF16))
        n = t - d_pv
        if 0 <= n < len(items):
            i = items[n][4]
            m_old, m_new = m_pair.pop(n)
            acc_ref[i] = jnp.exp2(m_old - m_new) * acc_ref[i] + weighted_values(items[n], p.pop(n))
            m_ref[i] = m_new


def _gqa_kernel(q_ref, k_lat_ref, k_ctx_ref, vt_lat_ref, vt_ctx_ref, o_ref, qx_ref, m_ref, acc_ref, *, tq):
    _attend_all(q_ref, k_lat_ref, k_ctx_ref, vt_lat_ref, vt_ctx_ref, qx_ref, m_ref, acc_ref,
                lambda i: 0, lambda i: i % 2, tq)
    for pair in range(A_GROUP):
        outs = [acc_ref[2 * pair + h, 0:HEAD_DIM] / acc_ref[2 * pair + h, HEAD_DIM:HEAD_DIM + 1] for h in range(2)]
        o_ref[:, pair * LANES:(pair + 1) * LANES] = jnp.concatenate(outs, axis=0).T.astype(BF16)


def _diff_lambda(lq1_ref, lk1_ref, lq2_ref, lk2_ref, lam_init):
    return (jnp.exp(jnp.sum(lq1_ref[...] * lk1_ref[...], axis=-1, keepdims=True))
            - jnp.exp(jnp.sum(lq2_ref[...] * lk2_ref[...], axis=-1, keepdims=True)) + lam_init)


def _diff_combine(o1, o2, lam, subg, lam_init):
    o = o1 - lam * o2
    o = o * lax.rsqrt(jnp.mean(o * o, axis=0, keepdims=True) + EPS)
    return o.T * subg * (1.0 - lam_init)


def _diff_kernel(q_ref, k_lat_ref, k_ctx_ref, vt_lat_ref, vt_ctx_ref, lq1_ref, lk1_ref, lq2_ref, lk2_ref,
                 subg_ref, o_ref, qx_ref, m_ref, acc_ref, *, tq, lam_init):
    _attend_all(q_ref, k_lat_ref, k_ctx_ref, vt_lat_ref, vt_ctx_ref, qx_ref, m_ref, acc_ref,
                lambda i: (i // 2) * LANES, lambda i: i // 2, tq)
    lam = _diff_lambda(lq1_ref, lk1_ref, lq2_ref, lk2_ref, lam_init)
    for head in range(C_HEADS):
        outs = [acc_ref[2 * head + h, 0:C_V_DIM] / acc_ref[2 * head + h, C_V_DIM:C_V_DIM + 1] for h in range(2)]
        o_ref[:, head * LANES:(head + 1) * LANES] = _diff_combine(
            outs[0], outs[1], lam, subg_ref[...], lam_init).astype(BF16)


def _attention_call(kernel, name, q, k_lat, k_ctx, vt_lat, vt_ctx, extra, *, tokens_per_batch, tq):
    nq = tokens_per_batch // tq
    kw = k_lat.shape[-1]
    g, dvx = vt_lat.shape[1], vt_lat.shape[3]
    bmap3 = lambda b, i: (b, 0, 0)
    bmap5 = lambda b, i: (b, 0, 0, 0, 0)
    in_specs = [
        pl.BlockSpec((tq, q.shape[1]), lambda b, i: (b * nq + i, 0)),
        pl.BlockSpec((1, k_lat.shape[1], kw), bmap3),
        pl.BlockSpec((1, k_ctx.shape[1], kw), bmap3),
        pl.BlockSpec((1, g, vt_lat.shape[2], dvx, vt_lat.shape[4]), bmap5),
        pl.BlockSpec((1, g, 1, dvx, vt_ctx.shape[4]), bmap5),
    ] + [_resident(e.shape) for e in extra]
    return pl.pallas_call(
        kernel,
        grid=(BATCH, nq),
        in_specs=in_specs,
        out_specs=pl.BlockSpec((tq, q.shape[1]), lambda b, i: (b * nq + i, 0)),
        out_shape=jax.ShapeDtypeStruct(q.shape, BF16),
        scratch_shapes=[pltpu.VMEM((N_PROBLEMS, tq, LANES), BF16), pltpu.VMEM((N_PROBLEMS, 1, tq), F32),
                        pltpu.VMEM((N_PROBLEMS, dvx, tq), F32)],
        compiler_params=_cparams("parallel", "arbitrary"),
        name=name,
    )(q, k_lat, k_ctx, vt_lat, vt_ctx, *extra)


def _attend_ctx(q_ext, k, vt):
    dn = (((1,), (1,)), ((), ()))
    st = lax.dot_general(k, q_ext, dn, preferred_element_type=F32)
    p = jnp.exp2(st - jnp.max(st, axis=0, keepdims=True)).astype(BF16)
    return _dot(vt, p)


def _gqa_ctx_kernel(q_ref, k_ref, vt_ref, o_ref, *, tq):
    for pair in range(A_GROUP):
        outs = []
        for half, q_ext in enumerate(_half_queries(q_ref, pair, tq)):
            acc = _attend_ctx(q_ext, k_ref[0], vt_ref[0, half, 0])
            outs.append(acc[0:HEAD_DIM] / acc[HEAD_DIM:HEAD_DIM + 1])
        o_ref[:, pair * LANES:(pair + 1) * LANES] = jnp.concatenate(outs, axis=0).T.astype(BF16)


def _diff_ctx_kernel(q_ref, k_ref, vt_ref, lq1_ref, lk1_ref, lq2_ref, lk2_ref, subg_ref, o_ref, *, tq, lam_init):
    lam = _diff_lambda(lq1_ref, lk1_ref, lq2_ref, lk2_ref, lam_init)
    for head in range(C_HEADS):
        outs = []
        for q_ext in _half_queries(q_ref, head, tq):
            acc = _attend_ctx(q_ext, k_ref[0, :, head * LANES:(head + 1) * LANES], vt_ref[0, head, 0])
            outs.append(acc[0:C_V_DIM] / acc[C_V_DIM:C_V_DIM + 1])
        o_ref[:, head * LANES:(head + 1) * LANES] = _diff_combine(
            outs[0], outs[1], lam, subg_ref[...], lam_init).astype(BF16)


def _ctx_attention_call(kernel, name, q, k, vt, extra):
    tq = CTX_LEN
    g, dvx = vt.shape[1], vt.shape[3]
    in_specs = [
        pl.BlockSpec((tq, q.shape[1]), lambda b: (b, 0)),
        pl.BlockSpec((1, tq, k.shape[-1]), lambda b: (b, 0, 0)),
        pl.BlockSpec((1, g, 1, dvx, tq), lambda b: (b, 0, 0, 0, 0)),
    ] + [_resident(e.shape) for e in extra]
    return pl.pallas_call(
        kernel,
        grid=(BATCH,),
        in_specs=in_specs,
        out_specs=pl.BlockSpec((tq, q.shape[1]), lambda b: (b, 0)),
        out_shape=jax.ShapeDtypeStruct(q.shape, BF16),
        compiler_params=_cparams("parallel"),
        name=name,
    )(q, k, vt, *extra)


CONV_HALO = 16


SUBLANES = 8
CONV_ROWS = 32


def _conv_kernel(prev_ref, cur_ref, next_ref, w_ref, b_ref, g_ref, beta_ref, o_ref, win_ref, sh_ref, *, tc, nblk):
    j = pl.program_id(1)
    win_ref[0:CONV_HALO, :] = jnp.where(j > 0, prev_ref[...], 0.0)
    win_ref[CONV_HALO:CONV_HALO + tc, :] = cur_ref[...]
    win_ref[CONV_HALO + tc:CONV_HALO + tc + CONV_HALO, :] = jnp.where(j < nblk - 1, next_ref[...], 0.0)
    span = sh_ref.shape[1]
    for s in range(1, SUBLANES):
        sh_ref[s - 1] = win_ref[s:s + span, :]
    first = CONV_HALO - B_CONV_W // 2
    nsub = CONV_ROWS // SUBLANES
    for r in range(0, tc, CONV_ROWS):
        accs = [jnp.zeros((SUBLANES, B_CHANNELS), F32) + b_ref[...]] * nsub
        for k in range(B_CONV_W):
            a, s = divmod(first + k, SUBLANES)
            wk = w_ref[k]
            for q in range(nsub):
                lo = r + (a + q) * SUBLANES
                src = win_ref[lo:lo + SUBLANES, :] if s == 0 else sh_ref[s - 1, lo:lo + SUBLANES, :]
                accs[q] = accs[q] + src * wk
        acc = jnp.concatenate(accs, axis=0)
        mu = jnp.mean(acc, axis=-1, keepdims=True)
        cen = acc - mu
        var = jnp.mean(cen * cen, axis=-1, keepdims=True)
        y = cen * lax.rsqrt(var + EPS) * g_ref[...] + beta_ref[...]
        o_ref[r:r + CONV_ROWS, :] = _silu(y).astype(BF16)


def _conv_call(yb, dw_w, dw_b, ln_g, ln_b, *, tokens_per_batch):
    p = tokens_per_batch
    tc = 256
    nblk = p // tc
    r = tc // CONV_HALO
    nh = p // CONV_HALO
    return pl.pallas_call(
        functools.partial(_conv_kernel, tc=tc, nblk=nblk),
        grid=(BATCH, nblk),
        in_specs=[
            pl.BlockSpec((CONV_HALO, B_CHANNELS), lambda b, j: (b * nh + jnp.maximum(j * r - 1, 0), 0)),
            pl.BlockSpec((tc, B_CHANNELS), lambda b, j: (b * nblk + j, 0)),
            pl.BlockSpec((CONV_HALO, B_CHANNELS), lambda b, j: (b * nh + jnp.minimum((j + 1) * r, nh - 1), 0)),
            _resident(dw_w.shape),
            _resident((1, B_CHANNELS)),
            _resident((1, B_CHANNELS)),
            _resident((1, B_CHANNELS)),
        ],
        out_specs=pl.BlockSpec((tc, B_CHANNELS), lambda b, j: (b * nblk + j, 0)),
        out_shape=jax.ShapeDtypeStruct((BATCH * p, B_CHANNELS), BF16),
        scratch_shapes=[pltpu.VMEM((tc + 2 * CONV_HALO, B_CHANNELS), F32),
                        pltpu.VMEM((SUBLANES - 1, tc + 2 * CONV_HALO - SUBLANES, B_CHANNELS), F32)],
        compiler_params=_cparams("parallel", "arbitrary"),
        name="conformer_conv",
    )(yb, yb, yb, dw_w, dw_b, ln_g, ln_b)


def _merge_kernel(x_ref, mod_ref, oa_ref, ob_ref, oc_ref, gt_ref, wpa_ref, wpb_ref, wpc_ref, wout_ref, o_ref):
    d = D_MODEL
    m = (gt_ref[:, 0:d].astype(F32) * _dot(oa_ref[...], wpa_ref[...])
         + gt_ref[:, d:2 * d].astype(F32) * _dot(ob_ref[...], wpb_ref[...])
         + gt_ref[:, 2 * d:3 * d].astype(F32) * _dot(oc_ref[...], wpc_ref[...]))
    y = _dot(m.astype(BF16), wout_ref[...])
    gate1 = mod_ref[0][:, 2 * d:3 * d]
    o_ref[...] = x_ref[...] + gate1 * y


def _merge_call(x, mod3, mod_row, oa, ob, oc, gt, wpa, wpb, wpc, wout, *, tm):
    t = x.shape[0]
    row = lambda i: (i, 0)
    return pl.pallas_call(
        _merge_kernel,
        grid=(t // tm,),
        in_specs=[
            pl.BlockSpec((tm, D_MODEL), row),
            pl.BlockSpec((1, 1, 6 * D_MODEL), lambda i: (mod_row(i), 0, 0)),
            pl.BlockSpec((tm, A_Q), row),
            pl.BlockSpec((tm, B_CHANNELS), row),
            pl.BlockSpec((tm, C_V), row),
            pl.BlockSpec((tm, N_BRANCH * D_MODEL), row),
            _resident(wpa.shape), _resident(wpb.shape), _resident(wpc.shape), _resident(wout.shape),
        ],
        out_specs=pl.BlockSpec((tm, D_MODEL), row),
        out_shape=jax.ShapeDtypeStruct(x.shape, F32),
        compiler_params=_cparams("parallel"),
        name="branch_merge",
    )(x, mod3, oa, ob, oc, gt, wpa, wpb, wpc, wout)


FFN_SPLIT = 2


def _ffn_kernel(x_ref, mod_ref, g2_ref, w1_ref, w3_ref, w2_ref, o_ref):
    d = D_MODEL
    mod = mod_ref[0]
    x = x_ref[...]
    h = _norm_mod(x, g2_ref[...], mod[:, 3 * d:4 * d], mod[:, 4 * d:5 * d]).astype(BF16)
    fc = D_FF // FFN_SPLIT
    y = jnp.zeros(x.shape, F32)
    for s in range(FFN_SPLIT):
        a = _dot(h, w1_ref[:, s * fc:(s + 1) * fc])
        b = _dot(h, w3_ref[:, s * fc:(s + 1) * fc])
        y = y + _dot((_silu(a) * b).astype(BF16), w2_ref[s * fc:(s + 1) * fc, :])
    o_ref[...] = x + mod[:, 5 * d:6 * d] * y


def _ffn_call(x, mod3, mod_row, g2, w1, w3, w2, *, tm):
    t = x.shape[0]
    row = lambda i: (i, 0)
    return pl.pallas_call(
        _ffn_kernel,
        grid=(t // tm,),
        in_specs=[
            pl.BlockSpec((tm, D_MODEL), row),
            pl.BlockSpec((1, 1, 6 * D_MODEL), lambda i: (mod_row(i), 0, 0)),
            _resident((1, D_MODEL)),
            _resident(w1.shape), _resident(w3.shape), _resident(w2.shape),
        ],
        out_specs=pl.BlockSpec((tm, D_MODEL), row),
        out_shape=jax.ShapeDtypeStruct(x.shape, F32),
        compiler_params=_cparams("parallel"),
        name="dense_swiglu",
    )(x, mod3, g2, w1, w3, w2)


MOE_F_TILE = 1792
MOE_TM = 512
TOP_K = 2


def _router_kernel(x_ref, mod_ref, g2_ref, wr_ref, br_ref, h_ref, sel_ref, wts_ref):
    d = D_MODEL
    mod = mod_ref[0]
    h = _norm_mod(x_ref[...], g2_ref[...], mod[:, 3 * d:4 * d], mod[:, 4 * d:5 * d])
    h_ref[...] = h
    logits = _dot_split(h, wr_ref[...]) + br_ref[...]
    lane = lax.broadcasted_iota(jnp.int32, logits.shape, 1)
    v1 = jnp.max(logits, axis=-1, keepdims=True)
    i1 = jnp.min(jnp.where(logits == v1, lane, LANES), axis=-1, keepdims=True)
    rest = jnp.where(lane == i1, -jnp.inf, logits)
    v2 = jnp.max(rest, axis=-1, keepdims=True)
    i2 = jnp.min(jnp.where(rest == v2, lane, LANES), axis=-1, keepdims=True)
    e2 = jnp.exp(v2 - v1)
    w1 = 1.0 / (1.0 + e2)
    sel_ref[...] = jnp.where(lane == 0, i1, jnp.where(lane == 1, i2, 0))
    wts_ref[...] = jnp.where(lane == 0, w1, jnp.where(lane == 1, e2 * w1, 0.0))


def _router_call(x, mod3, mod_row, g2, wr, br, *, tm):
    t = x.shape[0]
    row = lambda i: (i, 0)
    return pl.pallas_call(
        _router_kernel,
        grid=(t // tm,),
        in_specs=[
            pl.BlockSpec((tm, D_MODEL), row),
            pl.BlockSpec((1, 1, 6 * D_MODEL), lambda i: (mod_row(i), 0, 0)),
            _resident((1, D_MODEL)), _resident((D_MODEL, LANES)), _resident((1, LANES)),
        ],
        out_specs=(pl.BlockSpec((tm, D_MODEL), row), pl.BlockSpec((tm, LANES), row), pl.BlockSpec((tm, LANES), row)),
        out_shape=(jax.ShapeDtypeStruct((t, D_MODEL), F32), jax.ShapeDtypeStruct((t, LANES), jnp.int32),
                   jax.ShapeDtypeStruct((t, LANES), F32)),
        compiler_params=_cparams("parallel"),
        name="moe_router",
    )(x, mod3, g2, wr, br)


def _dispatch_plan(sel, n_rows):
    e_flat = sel.reshape(-1)
    onehot = (e_flat[:, None] == jnp.arange(N_EXPERTS, dtype=jnp.int32)[None, :]).astype(jnp.int32)
    csum = jnp.cumsum(onehot, axis=0)
    rank = jnp.take_along_axis(csum, e_flat[:, None], axis=1)[:, 0] - 1
    padded = (csum[-1] + MOE_TM - 1) // MOE_TM * MOE_TM
    end = jnp.cumsum(padded)
    pos = ((end - padded)[e_flat] + rank).astype(jnp.int32)
    row_token = jnp.zeros((n_rows,), jnp.int32).at[pos].set(
        jnp.arange(pos.shape[0], dtype=jnp.int32) // TOP_K, unique_indices=True, mode="promise_in_bounds")
    tile_start = jnp.arange(n_rows // MOE_TM, dtype=jnp.int32) * MOE_TM
    tile_expert = jnp.minimum(jnp.sum(tile_start[:, None] >= end[None, :], axis=1), N_EXPERTS - 1).astype(jnp.int32)
    n_tiles = (end[-1:] // MOE_TM).astype(jnp.int32)
    return row_token, pos, tile_expert, n_tiles


def _gather_rows(idx_ref, idx0, src_hbm, dst_ref, sem, row0, n, inline=False):
    def body(r, carry):
        src = src_hbm.at[pl.ds(idx_ref[idx0 + r], 1)]
        pltpu.make_async_copy(src, dst_ref.at[pl.ds(row0 + r, 1)], sem).start()
        return carry
    if inline:
        for r in range(n):
            body(r, 0)
    else:
        lax.fori_loop(0, n, body, 0, unroll=8)


def _wait_rows(src_hbm, dst_ref, sem):
    pltpu.make_async_copy(src_hbm.at[pl.ds(0, dst_ref.shape[0])], dst_ref, sem).wait()


def _expert_kernel(row_token_ref, tile_expert_ref, n_tiles_ref, h_hbm, w1_ref, w3_ref, w2_ref, y_ref,
                   xbuf_ref, acc_ref, sem_ref):
    j = pl.program_id(0)
    f = pl.program_id(1)
    nf = pl.num_programs(1)
    slot = j % 2
    valid = j < n_tiles_ref[0]

    @pl.when(jnp.logical_and(j == 0, f == 0))
    def _():
        _gather_rows(row_token_ref, 0, h_hbm, xbuf_ref.at[0], sem_ref.at[0], 0, MOE_TM)

    @pl.when(jnp.logical_and(valid, f == 0))
    def _():
        _wait_rows(h_hbm, xbuf_ref.at[slot], sem_ref.at[slot])

    def compute(gather_next):
        if gather_next:
            share = MOE_TM // (D_FF_EXPERT // MOE_F_TILE)
            _gather_rows(row_token_ref, (j + 1) * MOE_TM + f * share, h_hbm, xbuf_ref.at[1 - slot],
                         sem_ref.at[1 - slot], f * share, share, inline=True)
        x = xbuf_ref[slot].astype(BF16)
        z = (_silu(_dot(x, w1_ref[0])) * _dot(x, w3_ref[0])).astype(BF16)
        part = _dot(z, w2_ref[0])

        @pl.when(f == 0)
        def _():
            acc_ref[...] = part

        @pl.when(f > 0)
        def _():
            acc_ref[...] += part

        @pl.when(f == nf - 1)
        def _():
            y_ref[...] = acc_ref[...]

    has_next = j + 1 < n_tiles_ref[0]
    pl.when(has_next)(functools.partial(compute, True))
    pl.when(jnp.logical_and(valid, jnp.logical_not(has_next)))(functools.partial(compute, False))

    @pl.when(jnp.logical_and(jnp.logical_not(valid), f == nf - 1))
    def _():
        y_ref[...] = jnp.zeros(y_ref.shape, F32)


def _expert_call(h, row_token, tile_expert, n_tiles, w1, w3, w2):
    n_rows = row_token.shape[0]
    nt = n_rows // MOE_TM
    nf = D_FF_EXPERT // MOE_F_TILE

    def f_eff(j, f, n):
        return jnp.where(j < n[0], f, nf - 1)

    grid_spec = pltpu.PrefetchScalarGridSpec(
        num_scalar_prefetch=3,
        grid=(nt, nf),
        in_specs=[
            pl.BlockSpec(memory_space=pl.ANY),
            pl.BlockSpec((1, D_MODEL, MOE_F_TILE), lambda j, f, rt, te, n: (te[j], 0, f_eff(j, f, n))),
            pl.BlockSpec((1, D_MODEL, MOE_F_TILE), lambda j, f, rt, te, n: (te[j], 0, f_eff(j, f, n))),
            pl.BlockSpec((1, MOE_F_TILE, D_MODEL), lambda j, f, rt, te, n: (te[j], f_eff(j, f, n), 0)),
        ],
        out_specs=pl.BlockSpec((MOE_TM, D_MODEL), lambda j, f, rt, te, n: (j, 0)),
        scratch_shapes=[pltpu.VMEM((2, MOE_TM, D_MODEL), F32), pltpu.VMEM((MOE_TM, D_MODEL), F32),
                        pltpu.SemaphoreType.DMA((2,))],
    )
    return pl.pallas_call(
        _expert_kernel,
        grid_spec=grid_spec,
        out_shape=jax.ShapeDtypeStruct((n_rows, D_MODEL), F32),
        compiler_params=_cparams("arbitrary", "arbitrary"),
        name="moe_experts",
    )(row_token, tile_expert, n_tiles, h, w1, w3, w2)


def _combine_kernel(pos_ref, y_hbm, x_ref, mod_ref, wts_ref, fg_ref, o_ref, ybuf_ref, sem_ref, *, tc, final_norm):
    i = pl.program_id(0)
    slot = i % 2
    rows = TOP_K * tc

    @pl.when(i == 0)
    def _():
        _gather_rows(pos_ref, 0, y_hbm, ybuf_ref.at[0], sem_ref.at[0], 0, rows)

    _wait_rows(y_hbm, ybuf_ref.at[slot], sem_ref.at[slot])

    def finish(gather_next):
        if gather_next:
            _gather_rows(pos_ref, (i + 1) * rows, y_hbm, ybuf_ref.at[1 - slot], sem_ref.at[1 - slot], 0, rows,
                         inline=True)
        wts = wts_ref[...]
        moe = wts[:, 0:1] * ybuf_ref[slot, 0:tc, :] + wts[:, 1:2] * ybuf_ref[slot, tc:rows, :]
        y = x_ref[...] + mod_ref[0][:, 5 * D_MODEL:6 * D_MODEL] * moe
        if final_norm:
            y = y * lax.rsqrt(jnp.mean(y * y, axis=-1, keepdims=True) + EPS) * fg_ref[...]
        o_ref[...] = y

    has_next = i + 1 < pl.num_programs(0)
    pl.when(has_next)(functools.partial(finish, True))
    pl.when(jnp.logical_not(has_next))(functools.partial(finish, False))


def _combine_call(x, mod3, mod_row, y, pos, wts, fg, *, tc, final_norm):
    t = x.shape[0]
    row = lambda i, p: (i, 0)
    pos_tiles = pos.reshape(t // tc, tc, TOP_K).transpose(0, 2, 1).reshape(-1)
    grid_spec = pltpu.PrefetchScalarGridSpec(
        num_scalar_prefetch=1,
        grid=(t // tc,),
        in_specs=[
            pl.BlockSpec(memory_space=pl.ANY),
            pl.BlockSpec((tc, D_MODEL), row),
            pl.BlockSpec((1, 1, 6 * D_MODEL), lambda i, p: (mod_row(i), 0, 0)),
            pl.BlockSpec((tc, LANES), row),
            pl.BlockSpec((1, D_MODEL), lambda i, p: (0, 0)),
        ],
        out_specs=pl.BlockSpec((tc, D_MODEL), row),
        scratch_shapes=[pltpu.VMEM((2, TOP_K * tc, D_MODEL), F32), pltpu.SemaphoreType.DMA((2,))],
    )
    return pl.pallas_call(
        functools.partial(_combine_kernel, tc=tc, final_norm=final_norm),
        grid_spec=grid_spec,
        out_shape=jax.ShapeDtypeStruct(x.shape, F32),
        compiler_params=_cparams("arbitrary"),
        name="moe_combine",
    )(pos_tiles, y, x, mod3, wts, fg)


def _moe_call(x, mod3, mod_row, g2, wr, br, w1, w3, w2, fg, *, tm, final_norm):
    t = x.shape[0]
    h, sel, wts = _router_call(x, mod3, mod_row, g2, wr, br, tm=tm)
    n_rows = TOP_K * t + N_EXPERTS * MOE_TM
    row_token, pos, tile_expert, n_tiles = _dispatch_plan(sel[:, :TOP_K], n_rows)
    y = _expert_call(h, row_token, tile_expert, n_tiles, w1, w3, w2)
    tc = 256
    return _combine_call(x, mod3, lambda i: mod_row(i * tc // tm), y, pos, wts, fg, tc=tc, final_norm=final_norm)


def _final_norm_kernel(x_ref, g_ref, o_ref):
    x = x_ref[...]
    o_ref[...] = x * lax.rsqrt(jnp.mean(x * x, axis=-1, keepdims=True) + EPS) * g_ref[...]


def _final_norm_call(x, g, *, tm):
    row = lambda i: (i, 0)
    return pl.pallas_call(
        _final_norm_kernel,
        grid=(x.shape[0] // tm,),
        in_specs=[pl.BlockSpec((tm, D_MODEL), row), _resident((1, D_MODEL))],
        out_specs=pl.BlockSpec((tm, D_MODEL), row),
        out_shape=jax.ShapeDtypeStruct(x.shape, F32),
        compiler_params=_cparams("parallel"),
        name="final_rmsnorm",
    )(x, g)


def _rope_tables():
    rows = SEQ // GRID_W
    row = jnp.repeat(jnp.arange(rows), GRID_W)
    col = jnp.tile(jnp.arange(GRID_W), rows)
    n_freq = HEAD_DIM // 4
    inv = ROPE_THETA ** (-jnp.arange(n_freq, dtype=F32) / n_freq)
    ang = jnp.stack([row, col], axis=-1).astype(F32)[:, :, None] * inv
    cos, sin = jnp.cos(ang), jnp.sin(ang)
    cos64 = jnp.concatenate([cos[:, 0], cos[:, 0], cos[:, 1], cos[:, 1]], axis=-1)
    sin64 = jnp.concatenate([-sin[:, 0], sin[:, 0], -sin[:, 1], sin[:, 1]], axis=-1)
    return jnp.tile(cos64, (1, 2)), jnp.tile(sin64, (1, 2))


def _pair_heads(w, axis):
    shp = w.shape
    w = w.reshape(shp[:axis] + (A_KV_HEADS, A_GROUP, HEAD_DIM) + shp[axis + 1:])
    w = jnp.swapaxes(w, axis, axis + 1)
    return w.reshape(shp)


def kernel(x, c, ctx, c_ctx, w_mod, b_mod, norm1_g, norm2_g, w_in, b_gate, a_qn_g, a_kn_g, b_dw_w, b_dw_b,
           b_ln_g, b_ln_b, c_lq1, c_lk1, c_lq2, c_lk2, c_subln_g, w_pa, w_pb, w_pc, w_out, ffn_w1, ffn_w3,
           ffn_w2, moe_router, moe_router_b, moe_w1, moe_w3, moe_w2, final_g):
    d = D_MODEL
    t_lat = BATCH * SEQ
    xl = x.reshape(t_lat, d)
    xc = ctx.reshape(BATCH * CTX_LEN, d)

    cc = jnp.concatenate([c, c_ctx[None, :], jnp.zeros((3, d), F32)], axis=0)
    mod_all = _mod_call(cc, w_mod, b_mod)

    cos_t, sin_t = _rope_tables()
    head_id = np.arange(A_Q) // HEAD_DIM
    bd = jnp.asarray(head_id[:, None] == head_id[None, :], BF16)

    lat_tiles_per_batch = SEQ // KEY_CHUNK
    lat_row = lambda i: i // lat_tiles_per_batch
    ctx_row = lambda i: 4
    tm = KEY_CHUNK

    for i in range(DEPTH):
        last = i == DEPTH - 1
        lam_init = 0.8 - 0.6 * math.exp(-0.3 * i)
        mod3 = mod_all[i].reshape(8, 1, 6 * d)
        g1 = norm1_g[i].reshape(1, d)
        g2 = norm2_g[i].reshape(1, d)
        w_in_i = jnp.concatenate([_pair_heads(w_in[i][:, :A_Q], 1), w_in[i][:, A_Q:]], axis=1).astype(BF16)
        bg = b_gate[i].reshape(1, -1)
        qn = jnp.tile(a_qn_g[i], A_HEADS).reshape(1, A_Q)
        kn = jnp.tile(a_kn_g[i], A_KV_HEADS).reshape(1, A_KV)
        lam_vecs = [v[i].reshape(1, HEAD_DIM) for v in (c_lq1, c_lk1, c_lq2, c_lk2)]
        subg = c_subln_g[i].reshape(1, C_V_DIM)
        wpa = _pair_heads(w_pa[i], 0).astype(BF16)
        wpb, wpc, wout = w_pb[i].astype(BF16), w_pc[i].astype(BF16), w_out[i].astype(BF16)
        dw_w = jnp.broadcast_to(b_dw_w[i][:, None, :], (B_CONV_W, SUBLANES, B_CHANNELS))
        conv_p = (dw_w, b_dw_b[i].reshape(1, -1), b_ln_g[i].reshape(1, -1), b_ln_b[i].reshape(1, -1))

        proj = functools.partial(_inproj_call, g1=g1, w_in=w_in_i, b_gate=bg, qn=qn, kn=kn, bd=bd,
                                 cos_t=cos_t, sin_t=sin_t)
        qa, ka, vta, yb, qc, kc, vtc, gt = proj(xl, mod3, lat_row, tokens_per_batch=SEQ, rope=True)
        qa_c, ka_c, vta_c, yb_c, qc_c, kc_c, vtc_c, gt_c = proj(xc, mod3, ctx_row, tokens_per_batch=CTX_LEN,
                                                                rope=False)

        oa = _attention_call(functools.partial(_gqa_kernel, tq=256), "gqa_attention", qa, ka, ka_c, vta, vta_c,
                             [], tokens_per_batch=SEQ, tq=256)
        oc = _attention_call(functools.partial(_diff_kernel, tq=256, lam_init=lam_init), "diff_attention",
                             qc, kc, kc_c, vtc, vtc_c, lam_vecs + [subg], tokens_per_batch=SEQ, tq=256)
        ob = _conv_call(yb, *conv_p, tokens_per_batch=SEQ)
        xl = _merge_call(xl, mod3, lat_row, oa, ob, oc, gt, wpa, wpb, wpc, wout, tm=tm)

        if not last:
            oa_c = _ctx_attention_call(functools.partial(_gqa_ctx_kernel, tq=CTX_LEN), "gqa_attention_ctx",
                                       qa_c, ka_c, vta_c, [])
            oc_c = _ctx_attention_call(functools.partial(_diff_ctx_kernel, tq=CTX_LEN, lam_init=lam_init),
                                       "diff_attention_ctx", qc_c, kc_c, vtc_c, lam_vecs + [subg])
            ob_c = _conv_call(yb_c, *conv_p, tokens_per_batch=CTX_LEN)
            xc = _merge_call(xc, mod3, ctx_row, oa_c, ob_c, oc_c, gt_c, wpa, wpb, wpc, wout, tm=tm)

        j = i // 2
        if i % 2 == 0:
            ffn_w = (ffn_w1[j].astype(BF16), ffn_w3[j].astype(BF16), ffn_w2[j].astype(BF16))
            xl = _ffn_call(xl, mod3, lat_row, g2, *ffn_w, tm=tm)
            if last:
                xl = _final_norm_call(xl, final_g.reshape(1, d), tm=tm)
            else:
                xc = _ffn_call(xc, mod3, ctx_row, g2, *ffn_w, tm=tm)
        else:
            wr = jnp.zeros((d, LANES), F32).at[:, :N_EXPERTS].set(moe_router[j])
            br = jnp.full((1, LANES), -jnp.inf, F32).at[0, :N_EXPERTS].set(moe_router_b[j])
            moe_w = (moe_w1[j].astype(BF16), moe_w3[j].astype(BF16), moe_w2[j].astype(BF16))
            fg = final_g.reshape(1, d)
            xl = _moe_call(xl, mod3, lat_row, g2, wr, br, *moe_w, fg, tm=tm, final_norm=last)
            if not last:
                xc = _moe_call(xc, mod3, ctx_row, g2, wr, br, *moe_w, fg, tm=tm, final_norm=False)
    return xl.reshape(BATCH, SEQ, d)
```

```python
import functools
import math

import numpy as np
import jax
import jax.numpy as jnp
from jax import lax
from jax.experimental import pallas as pl
from jax.experimental.pallas import tpu as pltpu

F32 = jnp.float32
BF16 = jnp.bfloat16

D_MODEL = 1024
BATCH = 4
SEQ = 4096
DEPTH = 2
GRID_W = 64
CTX_LEN = 256
EPS = 1e-6
ROPE_THETA = 10000.0

HEAD_DIM = 64
A_HEADS = 8
A_KV_HEADS = 2
A_GROUP = 4
A_Q = 512
A_KV = 128
B_CHANNELS = 512
B_CONV_W = 31
C_HEADS = 4
C_V_DIM = 128
C_QK = 512
C_V = 512
N_BRANCH = 3
IN_SPLITS = [A_Q, A_KV, A_KV, 2 * B_CHANNELS, C_QK, C_QK, C_V, N_BRANCH * D_MODEL]
IN_WIDTH = sum(IN_SPLITS)
IN_OFF = [0] + [int(v) for v in np.cumsum(IN_SPLITS)]
D_FF = 2816
N_EXPERTS = 8
D_FF_EXPERT = 3584

Q_SCALE = HEAD_DIM ** -0.5 * 1.4426950408889634

LANES = 128
ONES_ROWS = 16
KEY_CHUNK = 512
VMEM_LIMIT = 56 * 1024 * 1024


def _cparams(*sem):
    return pltpu.CompilerParams(dimension_semantics=sem, vmem_limit_bytes=VMEM_LIMIT)


def _resident(shape):
    nd = len(shape)
    return pl.BlockSpec(shape, lambda *_: (0,) * nd, pipeline_mode=pl.Buffered(1))


def _sigmoid(v):
    return 1.0 / (1.0 + jnp.exp(-v))


def _silu(v):
    return v * _sigmoid(v)


def _norm_mod(x, g, shift, scale):
    ms = jnp.mean(x * x, axis=-1, keepdims=True)
    return (x * lax.rsqrt(ms + EPS) * g) * (1.0 + scale) + shift


def _dot(a, b):
    return jnp.dot(a, b, preferred_element_type=F32)


def _dot_split(a, b):
    ah, bh = a.astype(BF16), b.astype(BF16)
    al, bl = (a - ah.astype(F32)).astype(BF16), (b - bh.astype(F32)).astype(BF16)
    return _dot(ah, bh) + (_dot(ah, bl) + _dot(al, bh))


def _mod_kernel(c_ref, w_ref, b_ref, o_ref):
    s = _silu(c_ref[...])
    o_ref[0] = jnp.dot(s, w_ref[0], preferred_element_type=F32, precision=lax.Precision.HIGHEST) + b_ref[0]


def _mod_call(cc, w_mod, b_mod):
    n = 6 * D_MODEL
    tn = D_MODEL
    return pl.pallas_call(
        _mod_kernel,
        grid=(DEPTH, n // tn),
        in_specs=[
            pl.BlockSpec((8, D_MODEL), lambda i, j: (0, 0)),
            pl.BlockSpec((1, D_MODEL, tn), lambda i, j: (i, 0, j)),
            pl.BlockSpec((1, 1, tn), lambda i, j: (i, 0, j)),
        ],
        out_specs=pl.BlockSpec((1, 8, tn), lambda i, j: (i, 0, j)),
        out_shape=jax.ShapeDtypeStruct((DEPTH, 8, n), F32),
        compiler_params=_cparams("arbitrary", "arbitrary"),
        name="adaln_mod",
    )(cc, w_mod, b_mod.reshape(DEPTH, 1, n))


def _segment_mean_sq(u, bd):
    sq = u * u
    hi = sq.astype(BF16)
    lo = (sq - hi.astype(F32)).astype(BF16)
    return (_dot(hi, bd) + _dot(lo, bd)) * (1.0 / HEAD_DIM)


def _rope_group(xg, cos, sin, lo_mask):
    partner = jnp.where(lo_mask, pltpu.roll(xg, LANES - 16, 1), pltpu.roll(xg, 16, 1))
    return xg * cos + partner * sin


def _inproj_kernel(x_ref, mod_ref, g1_ref, w_ref, bgate_ref, qn_ref, kn_ref, bd_ref, cos_ref, sin_ref,
                   qa_ref, ka_ref, vta_ref, yb_ref, qc_ref, kc_ref, vtc_ref, gt_ref, *, rope, tm):
    mod = mod_ref[0]
    h = _norm_mod(x_ref[...], g1_ref[...], mod[:, 0:D_MODEL], mod[:, D_MODEL:2 * D_MODEL]).astype(BF16)

    def proj(c0, c1):
        return _dot(h, w_ref[:, c0:c1])

    if rope:
        cos = cos_ref[...]
        sin = sin_ref[...]
        lane = lax.broadcasted_iota(jnp.int32, (tm, LANES), 1)
        lo_mask = (lane % 32) < 16

    def rot(u, j):
        ug = u[:, j * LANES:(j + 1) * LANES]
        return _rope_group(ug, cos, sin, lo_mask) if rope else ug

    ones_pat = jnp.where(lax.broadcasted_iota(jnp.int32, (ONES_ROWS, tm), 0) == 0, 1.0, 0.0).astype(BF16)

    def finish_aq(u):
        u = u * lax.rsqrt(_segment_mean_sq(u, bd_ref[...]) + EPS) * (qn_ref[...] * Q_SCALE)
        for j in range(A_Q // LANES):
            qa_ref[:, j * LANES:(j + 1) * LANES] = rot(u, j).astype(BF16)

    def finish_ak(u):
        u = u * lax.rsqrt(_segment_mean_sq(u, bd_ref[0:A_KV, 0:A_KV]) + EPS) * kn_ref[...]
        ka_ref[0] = rot(u, 0).astype(BF16)

    def finish_av(u):
        ut = u.T
        for j in range(A_KV_HEADS):
            vta_ref[0, j, 0, 0:HEAD_DIM, :] = ut[j * HEAD_DIM:(j + 1) * HEAD_DIM].astype(BF16)
            vta_ref[0, j, 0, HEAD_DIM:HEAD_DIM + ONES_ROWS, :] = ones_pat

    def finish_b(u):
        yb_ref[...] = u[:, 0:B_CHANNELS] * _sigmoid(u[:, B_CHANNELS:2 * B_CHANNELS])

    def finish_cq(u):
        u = u * Q_SCALE
        for j in range(C_QK // LANES):
            qc_ref[:, j * LANES:(j + 1) * LANES] = rot(u, j).astype(BF16)

    def finish_ck(u):
        for j in range(C_QK // LANES):
            kc_ref[0, :, j * LANES:(j + 1) * LANES] = rot(u, j).astype(BF16)

    def finish_cv(u):
        ut = u.T
        for j in range(C_HEADS):
            vtc_ref[0, j, 0, 0:C_V_DIM, :] = ut[j * C_V_DIM:(j + 1) * C_V_DIM].astype(BF16)
            vtc_ref[0, j, 0, C_V_DIM:C_V_DIM + ONES_ROWS, :] = ones_pat

    def finish_gate(j, u):
        gt_ref[:, j * D_MODEL:(j + 1) * D_MODEL] = _sigmoid(
            u + bgate_ref[:, j * D_MODEL:(j + 1) * D_MODEL]).astype(BF16)

    stages = [((IN_OFF[g], IN_OFF[g + 1]), fin) for g, fin in enumerate(
        (finish_aq, finish_ak, finish_av, finish_b, finish_cq, finish_ck, finish_cv))]
    stages += [((IN_OFF[7] + j * D_MODEL, IN_OFF[7] + (j + 1) * D_MODEL), functools.partial(finish_gate, j))
               for j in range(N_BRANCH)]
    for cols, finish in stages:
        finish(proj(*cols))


def _inproj_call(x, mod3, mod_row, g1, w_in, b_gate, qn, kn, bd, cos_t, sin_t, *, tokens_per_batch, rope):
    p = tokens_per_batch
    tm = min(p, KEY_CHUNK)
    nblk = p // tm
    t = BATCH * p
    row = lambda i: (i, 0)
    vt_map = lambda i: (i // nblk, 0, i % nblk, 0, 0)
    key_map = lambda i: (i // nblk, i % nblk, 0)
    out_shape = (
        jax.ShapeDtypeStruct((t, A_Q), BF16),
        jax.ShapeDtypeStruct((BATCH, p, A_KV), BF16),
        jax.ShapeDtypeStruct((BATCH, A_KV_HEADS, nblk, HEAD_DIM + ONES_ROWS, tm), BF16),
        jax.ShapeDtypeStruct((t, B_CHANNELS), F32),
        jax.ShapeDtypeStruct((t, C_QK), BF16),
        jax.ShapeDtypeStruct((BATCH, p, C_QK), BF16),
        jax.ShapeDtypeStruct((BATCH, C_HEADS, nblk, C_V_DIM + ONES_ROWS, tm), BF16),
        jax.ShapeDtypeStruct((t, N_BRANCH * D_MODEL), BF16),
    )
    out_specs = (
        pl.BlockSpec((tm, A_Q), row),
        pl.BlockSpec((1, tm, A_KV), key_map),
        pl.BlockSpec((1, A_KV_HEADS, 1, HEAD_DIM + ONES_ROWS, tm), vt_map),
        pl.BlockSpec((tm, B_CHANNELS), row),
        pl.BlockSpec((tm, C_QK), row),
        pl.BlockSpec((1, tm, C_QK), key_map),
        pl.BlockSpec((1, C_HEADS, 1, C_V_DIM + ONES_ROWS, tm), vt_map),
        pl.BlockSpec((tm, N_BRANCH * D_MODEL), row),
    )
    in_specs = [
        pl.BlockSpec((tm, D_MODEL), row),
        pl.BlockSpec((1, 1, 6 * D_MODEL), lambda i: (mod_row(i), 0, 0)),
        _resident((1, D_MODEL)),
        _resident((D_MODEL, IN_WIDTH)),
        _resident((1, N_BRANCH * D_MODEL)),
        _resident((1, A_Q)),
        _resident((1, A_KV)),
        _resident((A_Q, A_Q)),
        pl.BlockSpec((tm, LANES), lambda i: (i % nblk, 0)),
        pl.BlockSpec((tm, LANES), lambda i: (i % nblk, 0)),
    ]
    return pl.pallas_call(
        functools.partial(_inproj_kernel, rope=rope, tm=tm),
        grid=(t // tm,),
        in_specs=in_specs,
        out_specs=out_specs,
        out_shape=out_shape,
        compiler_params=_cparams("parallel"),
        name="inproj_rope" if rope else "inproj_ctx",
    )(x, mod3, g1, w_in, b_gate, qn, kn, bd, cos_t, sin_t)


N_PROBLEMS = 8
STAGE_DELAYS = (3, 4, 5)
ATT_CHUNK = 256


def _half_queries(q_ref, group, rows):
    qg = q_ref[:, group * LANES:(group + 1) * LANES]
    lane = lax.broadcasted_iota(jnp.int32, (rows, LANES), 1)
    zero = jnp.zeros_like(qg)
    return jnp.where(lane < HEAD_DIM, qg, zero), jnp.where(lane >= HEAD_DIM, qg, zero)


def _attend_all(q_ref, k_lat_ref, k_ctx_ref, vt_lat_ref, vt_ctx_ref, qx_ref, m_ref, acc_ref, k_lane0, vgroup, tq):
    for g in range(N_PROBLEMS // 2):
        qx_ref[2 * g], qx_ref[2 * g + 1] = _half_queries(q_ref, g, tq)
    m_ref[...] = jnp.full(m_ref.shape, -jnp.inf, F32)
    acc_ref[...] = jnp.zeros(acc_ref.shape, F32)
    dn = (((1,), (1,)), ((), ()))

    n_lat = k_lat_ref.shape[1]
    chunks = [(k_lat_ref, vt_lat_ref, r0, ATT_CHUNK) for r0 in range(0, n_lat, ATT_CHUNK)]
    chunks.append((k_ctx_ref, vt_ctx_ref, 0, k_ctx_ref.shape[1]))
    items = [chunk + (i,) for chunk in chunks for i in range(N_PROBLEMS)]

    def scores(item):
        k_ref, _, r0, rows, i = item
        k = k_ref[0, r0:r0 + rows, k_lane0(i):k_lane0(i) + LANES]
        return lax.dot_general(k, qx_ref[i], dn, preferred_element_type=F32)

    def weighted_values(item, p):
        _, vt_ref, r0, rows, i = item
        blk = vt_ref.shape[4]
        width = min(blk, rows)
        out = None
        for b in range(0, rows, width):
            bi, off = divmod(r0 + b, blk)
            part = _dot(vt_ref[0, vgroup(i), bi, :, off:off + width], p[b:b + width])
            out = part if out is None else out + part
        return out

    d_max, d_exp, d_pv = STAGE_DELAYS
    st, m_pair, p = {}, {}, {}
    for t in range(len(items) + d_pv):
        if t < len(items):
            st[t] = scores(items[t])
        n = t - d_max
        if 0 <= n < len(items):
            i = items[n][4]
            m_old = m_ref[i]
            m_pair[n] = (m_old, jnp.maximum(m_old, jnp.max(st[n], axis=0, keepdims=True)))
        n = t - d_exp
        if 0 <= n < len(items):
            p[n] = jnp.exp2((st.pop(n) - m_pair[n][1]).astype(BF16))
        n = t - d_pv
        if 0 <= n < len(items):
            i = items[n][4]
            m_old, m_new = m_pair.pop(n)
            acc_ref[i] = jnp.exp2(m_old - m_new) * acc_ref[i] + weighted_values(items[n], p.pop(n))
            m_ref[i] = m_new


def _gqa_kernel(q_ref, k_lat_ref, k_ctx_ref, vt_lat_ref, vt_ctx_ref, o_ref, qx_ref, m_ref, acc_ref, *, tq):
    _attend_all(q_ref, k_lat_ref, k_ctx_ref, vt_lat_ref, vt_ctx_ref, qx_ref, m_ref, acc_ref,
                lambda i: 0, lambda i: i % 2, tq)
    for pair in range(A_GROUP):
        outs = [acc_ref[2 * pair + h, 0:HEAD_DIM] / acc_ref[2 * pair + h, HEAD_DIM:HEAD_DIM + 1] for h in range(2)]
        o_ref[:, pair * LANES:(pair + 1) * LANES] = jnp.concatenate(outs, axis=0).T.astype(BF16)


def _diff_lambda(lq1_ref, lk1_ref, lq2_ref, lk2_ref, lam_init):
    return (jnp.exp(jnp.sum(lq1_ref[...] * lk1_ref[...], axis=-1, keepdims=True))
            - jnp.exp(jnp.sum(lq2_ref[...] * lk2_ref[...], axis=-1, keepdims=True)) + lam_init)


def _diff_combine(o1, o2, lam, subg, lam_init):
    o = o1 - lam * o2
    o = o * lax.rsqrt(jnp.mean(o * o, axis=0, keepdims=True) + EPS)
    return o.T * subg * (1.0 - lam_init)


def _diff_kernel(q_ref, k_lat_ref, k_ctx_ref, vt_lat_ref, vt_ctx_ref, lq1_ref, lk1_ref, lq2_ref, lk2_ref,
                 subg_ref, o_ref, qx_ref, m_ref, acc_ref, *, tq, lam_init):
    _attend_all(q_ref, k_lat_ref, k_ctx_ref, vt_lat_ref, vt_ctx_ref, qx_ref, m_ref, acc_ref,
                lambda i: (i // 2) * LANES, lambda i: i // 2, tq)
    lam = _diff_lambda(lq1_ref, lk1_ref, lq2_ref, lk2_ref, lam_init)
    for head in range(C_HEADS):
        outs = [acc_ref[2 * head + h, 0:C_V_DIM] / acc_ref[2 * head + h, C_V_DIM:C_V_DIM + 1] for h in range(2)]
        o_ref[:, head * LANES:(head + 1) * LANES] = _diff_combine(
            outs[0], outs[1], lam, subg_ref[...], lam_init).astype(BF16)


def _attention_call(kernel, name, q, k_lat, k_ctx, vt_lat, vt_ctx, extra, *, tokens_per_batch, tq):
    nq = tokens_per_batch // tq
    kw = k_lat.shape[-1]
    g, dvx = vt_lat.shape[1], vt_lat.shape[3]
    bmap3 = lambda b, i: (b, 0, 0)
    bmap5 = lambda b, i: (b, 0, 0, 0, 0)
    in_specs = [
        pl.BlockSpec((tq, q.shape[1]), lambda b, i: (b * nq + i, 0)),
        pl.BlockSpec((1, k_lat.shape[1], kw), bmap3),
        pl.BlockSpec((1, k_ctx.shape[1], kw), bmap3),
        pl.BlockSpec((1, g, vt_lat.shape[2], dvx, vt_lat.shape[4]), bmap5),
        pl.BlockSpec((1, g, 1, dvx, vt_ctx.shape[4]), bmap5),
    ] + [_resident(e.shape) for e in extra]
    return pl.pallas_call(
        kernel,
        grid=(BATCH, nq),
        in_specs=in_specs,
        out_specs=pl.BlockSpec((tq, q.shape[1]), lambda b, i: (b * nq + i, 0)),
        out_shape=jax.ShapeDtypeStruct(q.shape, BF16),
        scratch_shapes=[pltpu.VMEM((N_PROBLEMS, tq, LANES), BF16), pltpu.VMEM((N_PROBLEMS, 1, tq), F32),
                        pltpu.VMEM((N_PROBLEMS, dvx, tq), F32)],
        compiler_params=_cparams("parallel", "arbitrary"),
        name=name,
    )(q, k_lat, k_ctx, vt_lat, vt_ctx, *extra)


def _attend_ctx(q_ext, k, vt):
    dn = (((1,), (1,)), ((), ()))
    st = lax.dot_general(k, q_ext, dn, preferred_element_type=F32)
    p = jnp.exp2(st - jnp.max(st, axis=0, keepdims=True)).astype(BF16)
    return _dot(vt, p)


def _gqa_ctx_kernel(q_ref, k_ref, vt_ref, o_ref, *, tq):
    for pair in range(A_GROUP):
        outs = []
        for half, q_ext in enumerate(_half_queries(q_ref, pair, tq)):
            acc = _attend_ctx(q_ext, k_ref[0], vt_ref[0, half, 0])
            outs.append(acc[0:HEAD_DIM] / acc[HEAD_DIM:HEAD_DIM + 1])
        o_ref[:, pair * LANES:(pair + 1) * LANES] = jnp.concatenate(outs, axis=0).T.astype(BF16)


def _diff_ctx_kernel(q_ref, k_ref, vt_ref, lq1_ref, lk1_ref, lq2_ref, lk2_ref, subg_ref, o_ref, *, tq, lam_init):
    lam = _diff_lambda(lq1_ref, lk1_ref, lq2_ref, lk2_ref, lam_init)
    for head in range(C_HEADS):
        outs = []
        for q_ext in _half_queries(q_ref, head, tq):
            acc = _attend_ctx(q_ext, k_ref[0, :, head * LANES:(head + 1) * LANES], vt_ref[0, head, 0])
            outs.append(acc[0:C_V_DIM] / acc[C_V_DIM:C_V_DIM + 1])
        o_ref[:, head * LANES:(head + 1) * LANES] = _diff_combine(
            outs[0], outs[1], lam, subg_ref[...], lam_init).astype(BF16)


def _ctx_attention_call(kernel, name, q, k, vt, extra):
    tq = CTX_LEN
    g, dvx = vt.shape[1], vt.shape[3]
    in_specs = [
        pl.BlockSpec((tq, q.shape[1]), lambda b: (b, 0)),
        pl.BlockSpec((1, tq, k.shape[-1]), lambda b: (b, 0, 0)),
        pl.BlockSpec((1, g, 1, dvx, tq), lambda b: (b, 0, 0, 0, 0)),
    ] + [_resident(e.shape) for e in extra]
    return pl.pallas_call(
        kernel,
        grid=(BATCH,),
        in_specs=in_specs,
        out_specs=pl.BlockSpec((tq, q.shape[1]), lambda b: (b, 0)),
        out_shape=jax.ShapeDtypeStruct(q.shape, BF16),
        compiler_params=_cparams("parallel"),
        name=name,
    )(q, k, vt, *extra)


CONV_HALO = 16


SUBLANES = 8
CONV_ROWS = 32


def _conv_kernel(prev_ref, cur_ref, next_ref, w_ref, b_ref, g_ref, beta_ref, o_ref, win_ref, sh_ref, *, tc, nblk):
    j = pl.program_id(1)
    win_ref[0:CONV_HALO, :] = jnp.where(j > 0, prev_ref[...], 0.0)
    win_ref[CONV_HALO:CONV_HALO + tc, :] = cur_ref[...]
    win_ref[CONV_HALO + tc:CONV_HALO + tc + CONV_HALO, :] = jnp.where(j < nblk - 1, next_ref[...], 0.0)
    span = sh_ref.shape[1]
    for s in range(1, SUBLANES):
        sh_ref[s - 1] = win_ref[s:s + span, :]
    first = CONV_HALO - B_CONV_W // 2
    nsub = CONV_ROWS // SUBLANES
    for r in range(0, tc, CONV_ROWS):
        accs = [jnp.zeros((SUBLANES, B_CHANNELS), F32) + b_ref[...]] * nsub
        for k in range(B_CONV_W):
            a, s = divmod(first + k, SUBLANES)
            wk = w_ref[k]
            for q in range(nsub):
                lo = r + (a + q) * SUBLANES
                src = win_ref[lo:lo + SUBLANES, :] if s == 0 else sh_ref[s - 1, lo:lo + SUBLANES, :]
                accs[q] = accs[q] + src * wk
        acc = jnp.concatenate(accs, axis=0)
        mu = jnp.mean(acc, axis=-1, keepdims=True)
        cen = acc - mu
        var = jnp.mean(cen * cen, axis=-1, keepdims=True)
        y = cen * lax.rsqrt(var + EPS) * g_ref[...] + beta_ref[...]
        o_ref[r:r + CONV_ROWS, :] = _silu(y).astype(BF16)


def _conv_call(yb, dw_w, dw_b, ln_g, ln_b, *, tokens_per_batch):
    p = tokens_per_batch
    tc = 256
    nblk = p // tc
    r = tc // CONV_HALO
    nh = p // CONV_HALO
    return pl.pallas_call(
        functools.partial(_conv_kernel, tc=tc, nblk=nblk),
        grid=(BATCH, nblk),
        in_specs=[
            pl.BlockSpec((CONV_HALO, B_CHANNELS), lambda b, j: (b * nh + jnp.maximum(j * r - 1, 0), 0)),
            pl.BlockSpec((tc, B_CHANNELS), lambda b, j: (b * nblk + j, 0)),
            pl.BlockSpec((CONV_HALO, B_CHANNELS), lambda b, j: (b * nh + jnp.minimum((j + 1) * r, nh - 1), 0)),
            _resident(dw_w.shape),
            _resident((1, B_CHANNELS)),
            _resident((1, B_CHANNELS)),
            _resident((1, B_CHANNELS)),
        ],
        out_specs=pl.BlockSpec((tc, B_CHANNELS), lambda b, j: (b * nblk + j, 0)),
        out_shape=jax.ShapeDtypeStruct((BATCH * p, B_CHANNELS), BF16),
        scratch_shapes=[pltpu.VMEM((tc + 2 * CONV_HALO, B_CHANNELS), F32),
                        pltpu.VMEM((SUBLANES - 1, tc + 2 * CONV_HALO - SUBLANES, B_CHANNELS), F32)],
        compiler_params=_cparams("parallel", "arbitrary"),
        name="conformer_conv",
    )(yb, yb, yb, dw_w, dw_b, ln_g, ln_b)


def _merge_kernel(x_ref, mod_ref, oa_ref, ob_ref, oc_ref, gt_ref, wpa_ref, wpb_ref, wpc_ref, wout_ref, o_ref):
    d = D_MODEL
    m = (gt_ref[:, 0:d].astype(F32) * _dot(oa_ref[...], wpa_ref[...])
         + gt_ref[:, d:2 * d].astype(F32) * _dot(ob_ref[...], wpb_ref[...])
         + gt_ref[:, 2 * d:3 * d].astype(F32) * _dot(oc_ref[...], wpc_ref[...]))
    y = _dot(m.astype(BF16), wout_ref[...])
    gate1 = mod_ref[0][:, 2 * d:3 * d]
    o_ref[...] = x_ref[...] + gate1 * y


def _merge_call(x, mod3, mod_row, oa, ob, oc, gt, wpa, wpb, wpc, wout, *, tm):
    t = x.shape[0]
    row = lambda i: (i, 0)
    return pl.pallas_call(
        _merge_kernel,
        grid=(t // tm,),
        in_specs=[
            pl.BlockSpec((tm, D_MODEL), row),
            pl.BlockSpec((1, 1, 6 * D_MODEL), lambda i: (mod_row(i), 0, 0)),
            pl.BlockSpec((tm, A_Q), row),
            pl.BlockSpec((tm, B_CHANNELS), row),
            pl.BlockSpec((tm, C_V), row),
            pl.BlockSpec((tm, N_BRANCH * D_MODEL), row),
            _resident(wpa.shape), _resident(wpb.shape), _resident(wpc.shape), _resident(wout.shape),
        ],
        out_specs=pl.BlockSpec((tm, D_MODEL), row),
        out_shape=jax.ShapeDtypeStruct(x.shape, F32),
        compiler_params=_cparams("parallel"),
        name="branch_merge",
    )(x, mod3, oa, ob, oc, gt, wpa, wpb, wpc, wout)


FFN_SPLIT = 2


def _ffn_kernel(x_ref, mod_ref, g2_ref, w1_ref, w3_ref, w2_ref, o_ref):
    d = D_MODEL
    mod = mod_ref[0]
    x = x_ref[...]
    h = _norm_mod(x, g2_ref[...], mod[:, 3 * d:4 * d], mod[:, 4 * d:5 * d]).astype(BF16)
    fc = D_FF // FFN_SPLIT
    y = jnp.zeros(x.shape, F32)
    for s in range(FFN_SPLIT):
        a = _dot(h, w1_ref[:, s * fc:(s + 1) * fc])
        b = _dot(h, w3_ref[:, s * fc:(s + 1) * fc])
        y = y + _dot((_silu(a) * b).astype(BF16), w2_ref[s * fc:(s + 1) * fc, :])
    o_ref[...] = x + mod[:, 5 * d:6 * d] * y


def _ffn_call(x, mod3, mod_row, g2, w1, w3, w2, *, tm):
    t = x.shape[0]
    row = lambda i: (i, 0)
    return pl.pallas_call(
        _ffn_kernel,
        grid=(t // tm,),
        in_specs=[
            pl.BlockSpec((tm, D_MODEL), row),
            pl.BlockSpec((1, 1, 6 * D_MODEL), lambda i: (mod_row(i), 0, 0)),
            _resident((1, D_MODEL)),
            _resident(w1.shape), _resident(w3.shape), _resident(w2.shape),
        ],
        out_specs=pl.BlockSpec((tm, D_MODEL), row),
        out_shape=jax.ShapeDtypeStruct(x.shape, F32),
        compiler_params=_cparams("parallel"),
        name="dense_swiglu",
    )(x, mod3, g2, w1, w3, w2)


MOE_F_TILE = 1792
MOE_TM = 512
TOP_K = 2


def _router_kernel(x_ref, mod_ref, g2_ref, wr_ref, br_ref, h_ref, sel_ref, wts_ref):
    d = D_MODEL
    mod = mod_ref[0]
    h = _norm_mod(x_ref[...], g2_ref[...], mod[:, 3 * d:4 * d], mod[:, 4 * d:5 * d])
    h_ref[...] = h
    logits = _dot_split(h, wr_ref[...]) + br_ref[...]
    lane = lax.broadcasted_iota(jnp.int32, logits.shape, 1)
    v1 = jnp.max(logits, axis=-1, keepdims=True)
    i1 = jnp.min(jnp.where(logits == v1, lane, LANES), axis=-1, keepdims=True)
    rest = jnp.where(lane == i1, -jnp.inf, logits)
    v2 = jnp.max(rest, axis=-1, keepdims=True)
    i2 = jnp.min(jnp.where(rest == v2, lane, LANES), axis=-1, keepdims=True)
    e2 = jnp.exp(v2 - v1)
    w1 = 1.0 / (1.0 + e2)
    sel_ref[...] = jnp.where(lane == 0, i1, jnp.where(lane == 1, i2, 0))
    wts_ref[...] = jnp.where(lane == 0, w1, jnp.where(lane == 1, e2 * w1, 0.0))


def _router_call(x, mod3, mod_row, g2, wr, br, *, tm):
    t = x.shape[0]
    row = lambda i: (i, 0)
    return pl.pallas_call(
        _router_kernel,
        grid=(t // tm,),
        in_specs=[
            pl.BlockSpec((tm, D_MODEL), row),
            pl.BlockSpec((1, 1, 6 * D_MODEL), lambda i: (mod_row(i), 0, 0)),
            _resident((1, D_MODEL)), _resident((D_MODEL, LANES)), _resident((1, LANES)),
        ],
        out_specs=(pl.BlockSpec((tm, D_MODEL), row), pl.BlockSpec((tm, LANES), row), pl.BlockSpec((tm, LANES), row)),
        out_shape=(jax.ShapeDtypeStruct((t, D_MODEL), F32), jax.ShapeDtypeStruct((t, LANES), jnp.int32),
                   jax.ShapeDtypeStruct((t, LANES), F32)),
        compiler_params=_cparams("parallel"),
        name="moe_router",
    )(x, mod3, g2, wr, br)


def _dispatch_plan(sel, n_rows):
    e_flat = sel.reshape(-1)
    onehot = (e_flat[:, None] == jnp.arange(N_EXPERTS, dtype=jnp.int32)[None, :]).astype(jnp.int32)
    csum = jnp.cumsum(onehot, axis=0)
    rank = jnp.take_along_axis(csum, e_flat[:, None], axis=1)[:, 0] - 1
    padded = (csum[-1] + MOE_TM - 1) // MOE_TM * MOE_TM
    end = jnp.cumsum(padded)
    pos = ((end - padded)[e_flat] + rank).astype(jnp.int32)
    row_token = jnp.zeros((n_rows,), jnp.int32).at[pos].set(
        jnp.arange(pos.shape[0], dtype=jnp.int32) // TOP_K, unique_indices=True, mode="promise_in_bounds")
    tile_start = jnp.arange(n_rows // MOE_TM, dtype=jnp.int32) * MOE_TM
    tile_expert = jnp.minimum(jnp.sum(tile_start[:, None] >= end[None, :], axis=1), N_EXPERTS - 1).astype(jnp.int32)
    n_tiles = (end[-1:] // MOE_TM).astype(jnp.int32)
    return row_token, pos, tile_expert, n_tiles


def _gather_rows(idx_ref, idx0, src_hbm, dst_ref, sem, row0, n, inline=False):
    def body(r, carry):
        src = src_hbm.at[pl.ds(idx_ref[idx0 + r], 1)]
        pltpu.make_async_copy(src, dst_ref.at[pl.ds(row0 + r, 1)], sem).start()
        return carry
    if inline:
        for r in range(n):
            body(r, 0)
    else:
        lax.fori_loop(0, n, body, 0, unroll=8)


def _wait_rows(src_hbm, dst_ref, sem):
    pltpu.make_async_copy(src_hbm.at[pl.ds(0, dst_ref.shape[0])], dst_ref, sem).wait()


def _expert_kernel(row_token_ref, tile_expert_ref, n_tiles_ref, h_hbm, w1_ref, w3_ref, w2_ref, y_ref,
                   xbuf_ref, acc_ref, sem_ref):
    j = pl.program_id(0)
    f = pl.program_id(1)
    nf = pl.num_programs(1)
    slot = j % 2
    valid = j < n_tiles_ref[0]

    @pl.when(jnp.logical_and(j == 0, f == 0))
    def _():
        _gather_rows(row_token_ref, 0, h_hbm, xbuf_ref.at[0], sem_ref.at[0], 0, MOE_TM)

    @pl.when(jnp.logical_and(valid, f == 0))
    def _():
        _wait_rows(h_hbm, xbuf_ref.at[slot], sem_ref.at[slot])

    def compute(gather_next):
        if gather_next:
            share = MOE_TM // (D_FF_EXPERT // MOE_F_TILE)
            _gather_rows(row_token_ref, (j + 1) * MOE_TM + f * share, h_hbm, xbuf_ref.at[1 - slot],
                         sem_ref.at[1 - slot], f * share, share, inline=True)
        x = xbuf_ref[slot].astype(BF16)
        z = (_silu(_dot(x, w1_ref[0])) * _dot(x, w3_ref[0])).astype(BF16)
        part = _dot(z, w2_ref[0])

        @pl.when(f == 0)
        def _():
            acc_ref[...] = part

        @pl.when(f > 0)
        def _():
            acc_ref[...] += part

        @pl.when(f == nf - 1)
        def _():
            y_ref[...] = acc_ref[...]

    has_next = j + 1 < n_tiles_ref[0]
    pl.when(has_next)(functools.partial(compute, True))
    pl.when(jnp.logical_and(valid, jnp.logical_not(has_next)))(functools.partial(compute, False))

    @pl.when(jnp.logical_and(jnp.logical_not(valid), f == nf - 1))
    def _():
        y_ref[...] = jnp.zeros(y_ref.shape, F32)


def _expert_call(h, row_token, tile_expert, n_tiles, w1, w3, w2):
    n_rows = row_token.shape[0]
    nt = n_rows // MOE_TM
    nf = D_FF_EXPERT // MOE_F_TILE

    def f_eff(j, f, n):
        return jnp.where(j < n[0], f, nf - 1)

    grid_spec = pltpu.PrefetchScalarGridSpec(
        num_scalar_prefetch=3,
        grid=(nt, nf),
        in_specs=[
            pl.BlockSpec(memory_space=pl.ANY),
            pl.BlockSpec((1, D_MODEL, MOE_F_TILE), lambda j, f, rt, te, n: (te[j], 0, f_eff(j, f, n))),
            pl.BlockSpec((1, D_MODEL, MOE_F_TILE), lambda j, f, rt, te, n: (te[j], 0, f_eff(j, f, n))),
            pl.BlockSpec((1, MOE_F_TILE, D_MODEL), lambda j, f, rt, te, n: (te[j], f_eff(j, f, n), 0)),
        ],
        out_specs=pl.BlockSpec((MOE_TM, D_MODEL), lambda j, f, rt, te, n: (j, 0)),
        scratch_shapes=[pltpu.VMEM((2, MOE_TM, D_MODEL), F32), pltpu.VMEM((MOE_TM, D_MODEL), F32),
                        pltpu.SemaphoreType.DMA((2,))],
    )
    return pl.pallas_call(
        _expert_kernel,
        grid_spec=grid_spec,
        out_shape=jax.ShapeDtypeStruct((n_rows, D_MODEL), F32),
        compiler_params=_cparams("arbitrary", "arbitrary"),
        name="moe_experts",
    )(row_token, tile_expert, n_tiles, h, w1, w3, w2)


def _combine_kernel(pos_ref, y_hbm, x_ref, mod_ref, wts_ref, fg_ref, o_ref, ybuf_ref, sem_ref, *, tc, final_norm):
    i = pl.program_id(0)
    slot = i % 2
    rows = TOP_K * tc

    @pl.when(i == 0)
    def _():
        _gather_rows(pos_ref, 0, y_hbm, ybuf_ref.at[0], sem_ref.at[0], 0, rows)

    @pl.when(i + 1 < pl.num_programs(0))
    def _():
        _gather_rows(pos_ref, (i + 1) * rows, y_hbm, ybuf_ref.at[1 - slot], sem_ref.at[1 - slot], 0, rows,
                     inline=True)

    _wait_rows(y_hbm, ybuf_ref.at[slot], sem_ref.at[slot])
    wts = wts_ref[...]
    moe = wts[:, 0:1] * ybuf_ref[slot, 0:tc, :] + wts[:, 1:2] * ybuf_ref[slot, tc:rows, :]
    y = x_ref[...] + mod_ref[0][:, 5 * D_MODEL:6 * D_MODEL] * moe
    if final_norm:
        y = y * lax.rsqrt(jnp.mean(y * y, axis=-1, keepdims=True) + EPS) * fg_ref[...]
    o_ref[...] = y


def _combine_call(x, mod3, mod_row, y, pos, wts, fg, *, tc, final_norm):
    t = x.shape[0]
    row = lambda i, p: (i, 0)
    pos_tiles = pos.reshape(t // tc, tc, TOP_K).transpose(0, 2, 1).reshape(-1)
    grid_spec = pltpu.PrefetchScalarGridSpec(
        num_scalar_prefetch=1,
        grid=(t // tc,),
        in_specs=[
            pl.BlockSpec(memory_space=pl.ANY),
            pl.BlockSpec((tc, D_MODEL), row),
            pl.BlockSpec((1, 1, 6 * D_MODEL), lambda i, p: (mod_row(i), 0, 0)),
            pl.BlockSpec((tc, LANES), row),
            pl.BlockSpec((1, D_MODEL), lambda i, p: (0, 0)),
        ],
        out_specs=pl.BlockSpec((tc, D_MODEL), row),
        scratch_shapes=[pltpu.VMEM((2, TOP_K * tc, D_MODEL), F32), pltpu.SemaphoreType.DMA((2,))],
    )
    return pl.pallas_call(
        functools.partial(_combine_kernel, tc=tc, final_norm=final_norm),
        grid_spec=grid_spec,
        out_shape=jax.ShapeDtypeStruct(x.shape, F32),
        compiler_params=_cparams("arbitrary"),
        name="moe_combine",
    )(pos_tiles, y, x, mod3, wts, fg)


def _moe_call(x, mod3, mod_row, g2, wr, br, w1, w3, w2, fg, *, tm, final_norm):
    t = x.shape[0]
    h, sel, wts = _router_call(x, mod3, mod_row, g2, wr, br, tm=tm)
    n_rows = TOP_K * t + N_EXPERTS * MOE_TM
    row_token, pos, tile_expert, n_tiles = _dispatch_plan(sel[:, :TOP_K], n_rows)
    y = _expert_call(h, row_token, tile_expert, n_tiles, w1, w3, w2)
    tc = 256
    return _combine_call(x, mod3, lambda i: mod_row(i * tc // tm), y, pos, wts, fg, tc=tc, final_norm=final_norm)


def _final_norm_kernel(x_ref, g_ref, o_ref):
    x = x_ref[...]
    o_ref[...] = x * lax.rsqrt(jnp.mean(x * x, axis=-1, keepdims=True) + EPS) * g_ref[...]


def _final_norm_call(x, g, *, tm):
    row = lambda i: (i, 0)
    return pl.pallas_call(
        _final_norm_kernel,
        grid=(x.shape[0] // tm,),
        in_specs=[pl.BlockSpec((tm, D_MODEL), row), _resident((1, D_MODEL))],
        out_specs=pl.BlockSpec((tm, D_MODEL), row),
        out_shape=jax.ShapeDtypeStruct(x.shape, F32),
        compiler_params=_cparams("parallel"),
        name="final_rmsnorm",
    )(x, g)


def _rope_tables():
    rows = SEQ // GRID_W
    row = jnp.repeat(jnp.arange(rows), GRID_W)
    col = jnp.tile(jnp.arange(GRID_W), rows)
    n_freq = HEAD_DIM // 4
    inv = ROPE_THETA ** (-jnp.arange(n_freq, dtype=F32) / n_freq)
    ang = jnp.stack([row, col], axis=-1).astype(F32)[:, :, None] * inv
    cos, sin = jnp.cos(ang), jnp.sin(ang)
    cos64 = jnp.concatenate([cos[:, 0], cos[:, 0], cos[:, 1], cos[:, 1]], axis=-1)
    sin64 = jnp.concatenate([-sin[:, 0], sin[:, 0], -sin[:, 1], sin[:, 1]], axis=-1)
    return jnp.tile(cos64, (1, 2)), jnp.tile(sin64, (1, 2))


def _pair_heads(w, axis):
    shp = w.shape
    w = w.reshape(shp[:axis] + (A_KV_HEADS, A_GROUP, HEAD_DIM) + shp[axis + 1:])
    w = jnp.swapaxes(w, axis, axis + 1)
    return w.reshape(shp)


def kernel(x, c, ctx, c_ctx, w_mod, b_mod, norm1_g, norm2_g, w_in, b_gate, a_qn_g, a_kn_g, b_dw_w, b_dw_b,
           b_ln_g, b_ln_b, c_lq1, c_lk1, c_lq2, c_lk2, c_subln_g, w_pa, w_pb, w_pc, w_out, ffn_w1, ffn_w3,
           ffn_w2, moe_router, moe_router_b, moe_w1, moe_w3, moe_w2, final_g):
    d = D_MODEL
    t_lat = BATCH * SEQ
    xl = x.reshape(t_lat, d)
    xc = ctx.reshape(BATCH * CTX_LEN, d)

    cc = jnp.concatenate([c, c_ctx[None, :], jnp.zeros((3, d), F32)], axis=0)
    mod_all = _mod_call(cc, w_mod, b_mod)

    cos_t, sin_t = _rope_tables()
    head_id = np.arange(A_Q) // HEAD_DIM
    bd = jnp.asarray(head_id[:, None] == head_id[None, :], BF16)

    lat_tiles_per_batch = SEQ // KEY_CHUNK
    lat_row = lambda i: i // lat_tiles_per_batch
    ctx_row = lambda i: 4
    tm = KEY_CHUNK

    for i in range(DEPTH):
        last = i == DEPTH - 1
        lam_init = 0.8 - 0.6 * math.exp(-0.3 * i)
        mod3 = mod_all[i].reshape(8, 1, 6 * d)
        g1 = norm1_g[i].reshape(1, d)
        g2 = norm2_g[i].reshape(1, d)
        w_in_i = jnp.concatenate([_pair_heads(w_in[i][:, :A_Q], 1), w_in[i][:, A_Q:]], axis=1).astype(BF16)
        bg = b_gate[i].reshape(1, -1)
        qn = jnp.tile(a_qn_g[i], A_HEADS).reshape(1, A_Q)
        kn = jnp.tile(a_kn_g[i], A_KV_HEADS).reshape(1, A_KV)
        lam_vecs = [v[i].reshape(1, HEAD_DIM) for v in (c_lq1, c_lk1, c_lq2, c_lk2)]
        subg = c_subln_g[i].reshape(1, C_V_DIM)
        wpa = _pair_heads(w_pa[i], 0).astype(BF16)
        wpb, wpc, wout = w_pb[i].astype(BF16), w_pc[i].astype(BF16), w_out[i].astype(BF16)
        dw_w = jnp.broadcast_to(b_dw_w[i][:, None, :], (B_CONV_W, SUBLANES, B_CHANNELS))
        conv_p = (dw_w, b_dw_b[i].reshape(1, -1), b_ln_g[i].reshape(1, -1), b_ln_b[i].reshape(1, -1))

        proj = functools.partial(_inproj_call, g1=g1, w_in=w_in_i, b_gate=bg, qn=qn, kn=kn, bd=bd,
                                 cos_t=cos_t, sin_t=sin_t)
        qa, ka, vta, yb, qc, kc, vtc, gt = proj(xl, mod3, lat_row, tokens_per_batch=SEQ, rope=True)
        qa_c, ka_c, vta_c, yb_c, qc_c, kc_c, vtc_c, gt_c = proj(xc, mod3, ctx_row, tokens_per_batch=CTX_LEN,
                                                                rope=False)

        oa = _attention_call(functools.partial(_gqa_kernel, tq=256), "gqa_attention", qa, ka, ka_c, vta, vta_c,
                             [], tokens_per_batch=SEQ, tq=256)
        oc = _attention_call(functools.partial(_diff_kernel, tq=256, lam_init=lam_init), "diff_attention",
                             qc, kc, kc_c, vtc, vtc_c, lam_vecs + [subg], tokens_per_batch=SEQ, tq=256)
        ob = _conv_call(yb, *conv_p, tokens_per_batch=SEQ)
        xl = _merge_call(xl, mod3, lat_row, oa, ob, oc, gt, wpa, wpb, wpc, wout, tm=tm)

        if not last:
            oa_c = _ctx_attention_call(functools.partial(_gqa_ctx_kernel, tq=CTX_LEN), "gqa_attention_ctx",
                                       qa_c, ka_c, vta_c, [])
            oc_c = _ctx_attention_call(functools.partial(_diff_ctx_kernel, tq=CTX_LEN, lam_init=lam_init),
                                       "diff_attention_ctx", qc_c, kc_c, vtc_c, lam_vecs + [subg])
            ob_c = _conv_call(yb_c, *conv_p, tokens_per_batch=CTX_LEN)
            xc = _merge_call(xc, mod3, ctx_row, oa_c, ob_c, oc_c, gt_c, wpa, wpb, wpc, wout, tm=tm)

        j = i // 2
        if i % 2 == 0:
            ffn_w = (ffn_w1[j].astype(BF16), ffn_w3[j].astype(BF16), ffn_w2[j].astype(BF16))
            xl = _ffn_call(xl, mod3, lat_row, g2, *ffn_w, tm=tm)
            if last:
                xl = _final_norm_call(xl, final_g.reshape(1, d), tm=tm)
            else:
                xc = _ffn_call(xc, mod3, ctx_row, g2, *ffn_w, tm=tm)
        else:
            wr = jnp.zeros((d, LANES), F32).at[:, :N_EXPERTS].set(moe_router[j])
            br = jnp.full((1, LANES), -jnp.inf, F32).at[0, :N_EXPERTS].set(moe_router_b[j])
            moe_w = (moe_w1[j].astype(BF16), moe_w3[j].astype(BF16), moe_w2[j].astype(BF16))
            fg = final_g.reshape(1, d)
            xl = _moe_call(xl, mod3, lat_row, g2, wr, br, *moe_w, fg, tm=tm, final_norm=last)
            if not last:
                xc = _moe_call(xc, mod3, ctx_row, g2, wr, br, *moe_w, fg, tm=tm, final_norm=False)
    return xl.reshape(BATCH, SEQ, d)
```
